```python
import jax, jax.numpy as jnp
from jax import lax
import numpy as np

D_MODEL = 1024
BATCH = 8
SEQ = 8192
DEPTH = 2

D_FF = 2816
FFN_RES_WEIGHT = 0.5
ATT_HEADS = 8
ATT_HEAD_DIM = 64
ATT_WIDTH = ATT_HEADS * ATT_HEAD_DIM
KV_DIM = ATT_HEAD_DIM
IDX_HEADS = 8
IDX_DIM = 32
TOPK_MAX = 256
Q_BLOCK = 128
SC_WIDTH = D_MODEL - ATT_WIDTH
SC_GROUPS = 8
SC_KERNEL = 3
CONF_WIDTH = D_MODEL
CONF_KERNEL = 31
ROPE_THETA = 500000.0
ROT_FRACTION = 4
NORM_EPS = 1e-6

HYB_SPLIT_SIZES = (ATT_WIDTH, KV_DIM, KV_DIM, IDX_HEADS * IDX_DIM, IDX_DIM, IDX_HEADS,
                   SC_WIDTH, SC_WIDTH, SC_WIDTH)
HYB_COLS = ATT_WIDTH + 2 * KV_DIM + IDX_HEADS * IDX_DIM + IDX_DIM + IDX_HEADS + 3 * SC_WIDTH

kernel_name = "hybrid_dsa_shortconv_conformer_macaron"


def rms_norm(x, g):
    xf = x.astype(jnp.float32)
    y = xf * lax.rsqrt(jnp.mean(xf * xf, axis=-1, keepdims=True) + NORM_EPS)
    return (y * g.astype(jnp.float32)).astype(x.dtype)


def layer_norm(x, g, b):
    xf = x.astype(jnp.float32)
    mu = jnp.mean(xf, axis=-1, keepdims=True)
    xc = xf - mu
    var = jnp.mean(xc * xc, axis=-1, keepdims=True)
    y = xc * lax.rsqrt(var + NORM_EPS) * g.astype(jnp.float32) + b.astype(jnp.float32)
    return y.astype(x.dtype)


def half_step_swiglu(h, g, w_gate, w_up, w_down):
    hn = rms_norm(h, g)
    return h + FFN_RES_WEIGHT * ((jax.nn.silu(hn @ w_gate) * (hn @ w_up)) @ w_down)


def rope_tables(seq, rot_dim):
    half = rot_dim // 2
    inv_freq = ROPE_THETA ** (-jnp.arange(half, dtype=jnp.float32) / half)
    ang = jnp.arange(seq, dtype=jnp.float32)[:, None] * inv_freq[None, :]
    return jnp.cos(ang), jnp.sin(ang)


def partial_rope(x, cos, sin):
    half = cos.shape[-1]
    shape = (1, cos.shape[0]) + (1,) * (x.ndim - 3) + (half,)
    c = cos.reshape(shape).astype(x.dtype)
    s = sin.reshape(shape).astype(x.dtype)
    x1 = x[..., :half]
    x2 = x[..., half:2 * half]
    return jnp.concatenate([x1 * c - x2 * s, x2 * c + x1 * s, x[..., 2 * half:]], axis=-1)


def causal_depthwise_conv(x, w):
    k, c = w.shape
    return lax.conv_general_dilated(
        x, w[:, None, :].astype(x.dtype), window_strides=(1,), padding=[(k - 1, 0)],
        dimension_numbers=('NWC', 'WIO', 'NWC'), feature_group_count=c)


def dsa_sparse_attention(q, k, v, q_idx, k_idx, w_idx):
    b, s, n_heads, dh = q.shape
    topk = min(TOPK_MAX, s // 4)
    n_blk = s // Q_BLOCK
    key_pos = jnp.arange(s)
    k_idx_f = k_idx.astype(jnp.float32)

    def to_blocks(a):
        return a.reshape((b, n_blk, Q_BLOCK) + a.shape[2:]).swapaxes(0, 1)

    def block(args):
        start, qb, qib, wib = args
        q_pos = start + jnp.arange(Q_BLOCK)
        dots = jnp.einsum('bqhd,bkd->bqhk', qib.astype(jnp.float32), k_idx_f) * (IDX_DIM ** -0.5)
        score = jnp.einsum('bqh,bqhk->bqk', wib.astype(jnp.float32), jax.nn.relu(dots))
        causal = key_pos[None, :] <= q_pos[:, None]
        score = jnp.where(causal[None], score, -jnp.inf)
        _, sel = lax.top_k(score, topk)
        valid = sel <= q_pos[None, :, None]
        k_sel = jax.vmap(lambda kk, ii: kk[ii])(k, sel)
        v_sel = jax.vmap(lambda vv, ii: vv[ii])(v, sel)
        logits = jnp.einsum('bqhd,bqkd->bqhk', qb, k_sel).astype(jnp.float32) * (dh ** -0.5)
        logits = jnp.where(valid[:, :, None, :], logits, -jnp.inf)
        p = jax.nn.softmax(logits, axis=-1).astype(v.dtype)
        return jnp.einsum('bqhk,bqkd->bqhd', p, v_sel)

    starts = jnp.arange(n_blk) * Q_BLOCK
    out = lax.map(block, (starts, to_blocks(q), to_blocks(q_idx), to_blocks(w_idx)))
    return out.swapaxes(0, 1).reshape(b, s, n_heads, dh)


def hybrid_dsa_shortconv_mixer(hn, w_in, conv_w, w_out, cos_a, sin_a, cos_i, sin_i):
    b, s, _ = hn.shape
    z = hn @ w_in
    offsets = []
    acc = 0
    for size in HYB_SPLIT_SIZES[:-1]:
        acc += size
        offsets.append(acc)
    q, k, v, qi, ki, wi, gate_b, gate_c, u = jnp.split(z, offsets, axis=-1)
    q = partial_rope(q.reshape(b, s, ATT_HEADS, ATT_HEAD_DIM), cos_a, sin_a)
    k = partial_rope(k, cos_a, sin_a)
    qi = partial_rope(qi.reshape(b, s, IDX_HEADS, IDX_DIM), cos_i, sin_i)
    ki = partial_rope(ki, cos_i, sin_i)
    wi = wi * (IDX_HEADS ** -0.5)
    y_attn = dsa_sparse_attention(q, k, v, qi, ki, wi).reshape(b, s, ATT_WIDTH)
    y_conv = gate_b * causal_depthwise_conv(gate_c * u, conv_w)
    return jnp.concatenate([y_attn, y_conv], axis=-1) @ w_out


def conformer_conv_module(hn, w_pw1, b_pw1, conv_w, conv_b, ln_g, ln_b, w_pw2, b_pw2):
    a, gate = jnp.split(hn @ w_pw1 + b_pw1, 2, axis=-1)
    u = a * jax.nn.sigmoid(gate)
    u = causal_depthwise_conv(u, conv_w) + conv_b
    u = jax.nn.silu(layer_norm(u, ln_g, ln_b))
    return u @ w_pw2 + b_pw2


def setup_inputs(seed: int = 0) -> dict:
    key = jax.random.key(seed)
    ks = jax.random.split(key, 20)
    n_even = (DEPTH + 1) // 2
    n_odd = DEPTH // 2
    f32 = jnp.float32

    def nrm(k, shape, scale):
        return jax.random.normal(k, shape, f32) * scale

    return {
        "x": nrm(ks[0], (BATCH, SEQ, D_MODEL), 1.0),
        "ffn_norm": 1.0 + nrm(ks[1], (DEPTH, 2, D_MODEL), 0.02),
        "ffn_w_gate": nrm(ks[2], (DEPTH, 2, D_MODEL, D_FF), D_MODEL ** -0.5),
        "ffn_w_up": nrm(ks[3], (DEPTH, 2, D_MODEL, D_FF), D_MODEL ** -0.5),
        "ffn_w_down": nrm(ks[4], (DEPTH, 2, D_FF, D_MODEL), D_FF ** -0.5),
        "mix_norm": 1.0 + nrm(ks[5], (DEPTH, D_MODEL), 0.02),
        "hyb_w_in": nrm(ks[6], (n_even, D_MODEL, HYB_COLS), D_MODEL ** -0.5),
        "hyb_conv_w": nrm(ks[7], (n_even, SC_KERNEL, SC_WIDTH), SC_KERNEL ** -0.5),
        "hyb_w_out": nrm(ks[8], (n_even, ATT_WIDTH + SC_WIDTH, D_MODEL), (ATT_WIDTH + SC_WIDTH) ** -0.5),
        "conf_w_pw1": nrm(ks[9], (n_odd, D_MODEL, 2 * CONF_WIDTH), D_MODEL ** -0.5),
        "conf_b_pw1": nrm(ks[10], (n_odd, 2 * CONF_WIDTH), 0.02),
        "conf_conv_w": nrm(ks[11], (n_odd, CONF_KERNEL, CONF_WIDTH), CONF_KERNEL ** -0.5),
        "conf_conv_b": nrm(ks[12], (n_odd, CONF_WIDTH), 0.02),
        "conf_ln_g": 1.0 + nrm(ks[13], (n_odd, CONF_WIDTH), 0.02),
        "conf_ln_b": nrm(ks[14], (n_odd, CONF_WIDTH), 0.02),
        "conf_w_pw2": nrm(ks[15], (n_odd, CONF_WIDTH, D_MODEL), CONF_WIDTH ** -0.5),
        "conf_b_pw2": nrm(ks[16], (n_odd, D_MODEL), 0.02),
        "final_norm": 1.0 + nrm(ks[17], (D_MODEL,), 0.02),
    }


def reference(x, ffn_norm, ffn_w_gate, ffn_w_up, ffn_w_down, mix_norm, hyb_w_in, hyb_conv_w,
              hyb_w_out, conf_w_pw1, conf_b_pw1, conf_conv_w, conf_conv_b, conf_ln_g, conf_ln_b,
              conf_w_pw2, conf_b_pw2, final_norm):
    s = x.shape[1]
    cos_a, sin_a = rope_tables(s, ATT_HEAD_DIM // ROT_FRACTION)
    cos_i, sin_i = rope_tables(s, IDX_DIM // ROT_FRACTION)
    h = x
    for layer in range(DEPTH):
        h = half_step_swiglu(h, ffn_norm[layer, 0], ffn_w_gate[layer, 0], ffn_w_up[layer, 0],
                             ffn_w_down[layer, 0])
        hn = rms_norm(h, mix_norm[layer])
        if layer % 2 == 0:
            e = layer // 2
            h = h + hybrid_dsa_shortconv_mixer(hn, hyb_w_in[e], hyb_conv_w[e], hyb_w_out[e],
                                               cos_a, sin_a, cos_i, sin_i)
        else:
            o = layer // 2
            h = h + conformer_conv_module(hn, conf_w_pw1[o], conf_b_pw1[o], conf_conv_w[o],
                                          conf_conv_b[o], conf_ln_g[o], conf_ln_b[o],
                                          conf_w_pw2[o], conf_b_pw2[o])
        h = half_step_swiglu(h, ffn_norm[layer, 1], ffn_w_gate[layer, 1], ffn_w_up[layer, 1],
                             ffn_w_down[layer, 1])
    return rms_norm(h, final_norm)
```

```python
import functools

import jax
import jax.numpy as jnp
from jax import lax
from jax.experimental import pallas as pl
from jax.experimental.pallas import tpu as pltpu

F32 = jnp.float32
BF16 = jnp.bfloat16
I32 = jnp.int32

D_MODEL = 1024
D_FF = 2816
FFN_RES_WEIGHT = 0.5
ATT_HEADS = 8
ATT_HEAD_DIM = 64
ATT_WIDTH = ATT_HEADS * ATT_HEAD_DIM
IDX_HEADS = 8
IDX_DIM = 32
IDX_WIDTH = IDX_HEADS * IDX_DIM
TOPK_MAX = 256
Q_BLOCK = 128
SC_WIDTH = D_MODEL - ATT_WIDTH
SC_KERNEL = 3
CONF_WIDTH = D_MODEL
CONF_KERNEL = 31
ROPE_THETA = 500000.0
ROT_FRACTION = 4
NORM_EPS = 1e-6

LANES = 128
SUBLANES = 8
VMEM_LIMIT = 56 * 1024 * 1024

A_Q = 0
A_QI = A_Q + ATT_WIDTH
A_K = A_QI + IDX_WIDTH
A_V = A_K + LANES
A_KI = A_V + LANES
A_WI = A_KI + LANES
A_COLS = A_WI + LANES

KEY_CHUNK = 512
INT_MIN = -2 ** 31
NEG_BIG = -1e30
CONF_HALO = 32
SC_HALO = 8


def _dot(a, b):
    return jnp.dot(a, b, preferred_element_type=F32)


def _dot_nt(a, b):
    return lax.dot_general(a, b, (((1,), (1,)), ((), ())), preferred_element_type=F32)


def _rms(x, g):
    return x * lax.rsqrt(jnp.mean(x * x, axis=-1, keepdims=True) + NORM_EPS) * g


def _sigmoid(x):
    return 1.0 / (1.0 + jnp.exp(-x))


def _ffn_kernel(*refs, n_pre, has_bias, final_norm):
    it = iter(refs)
    x_ref = next(it)
    pre = [(next(it), next(it)) for _ in range(n_pre)]
    bias_ref = next(it) if has_bias else None
    g_ref, wg_ref, wu_ref, wd_ref = next(it), next(it), next(it), next(it)
    gf_ref = next(it) if final_norm else None
    o_ref = next(it)

    x = x_ref[...]
    if n_pre:
        upd = _dot(pre[0][0][...], pre[0][1][...])
        for a_ref, w_ref in pre[1:]:
            upd = upd + _dot(a_ref[...], w_ref[...])
        if has_bias:
            upd = upd + bias_ref[...]
        x = x + upd
    xn = _rms(x, g_ref[...]).astype(BF16)
    gate = _dot(xn, wg_ref[...])
    up = _dot(xn, wu_ref[...])
    mid = (gate * _sigmoid(gate) * up).astype(BF16)
    y = x + FFN_RES_WEIGHT * _dot(mid, wd_ref[...])
    if final_norm:
        y = _rms(y, gf_ref[...])
    o_ref[...] = y


def _const_spec(shape):
    nd = len(shape)
    return pl.BlockSpec(shape, lambda *_: (0,) * nd, pipeline_mode=pl.Buffered(1))


def _ffn(x2d, g, wg, wu, wd, pre=(), bias=None, final_g=None, tm=512):
    n = x2d.shape[0]
    tm = min(tm, n)
    row = lambda w: pl.BlockSpec((tm, w), lambda i: (i, 0))
    args, specs = [x2d], [row(D_MODEL)]
    for act, w in pre:
        args += [act, w]
        specs += [row(act.shape[1]), _const_spec(w.shape)]
    if bias is not None:
        args.append(bias)
        specs.append(_const_spec(bias.shape))
    args += [g, wg, wu, wd]
    specs += [_const_spec(g.shape), _const_spec(wg.shape), _const_spec(wu.shape), _const_spec(wd.shape)]
    if final_g is not None:
        args.append(final_g)
        specs.append(_const_spec(final_g.shape))
    kern = functools.partial(_ffn_kernel, n_pre=len(pre), has_bias=bias is not None,
                             final_norm=final_g is not None)
    return pl.pallas_call(
        kern, grid=(n // tm,), in_specs=specs, out_specs=row(D_MODEL),
        out_shape=jax.ShapeDtypeStruct((n, D_MODEL), F32),
        compiler_params=pltpu.CompilerParams(dimension_semantics=("arbitrary",),
                                             vmem_limit_bytes=VMEM_LIMIT),
        name="ffn",
    )(*args)


def _rope(z, cos, sin, half, period):
    width = z.shape[-1]
    lane = lax.broadcasted_iota(I32, z.shape, 1)
    first = (lane & (period - 1)) < half
    partner = jnp.where(first, pltpu.roll(z, width - half, 1), pltpu.roll(z, half, 1))
    reps = width // LANES
    if reps > 1:
        cos = jnp.concatenate([cos] * reps, axis=1)
        sin = jnp.concatenate([sin] * reps, axis=1)
    return z * cos + partner * sin


def _mix_in_kernel(x_ref, g_ref, wa_ref, wb_ref, cw_ref, tab_ref,
                   q_ref, qi_ref, k_ref, v_ref, ki_ref, wi_ref, yc_ref,
                   cbuf_ref, carry_ref):
    s = pl.program_id(0)
    b = pl.program_id(1)
    tm = x_ref.shape[1]
    xn = _rms(x_ref[0], g_ref[...]).astype(BF16)
    za = _dot(xn, wa_ref[...])
    cos_a, sin_a = tab_ref[:, 0:LANES], tab_ref[:, LANES:2 * LANES]
    cos_i, sin_i = tab_ref[:, 2 * LANES:3 * LANES], tab_ref[:, 3 * LANES:4 * LANES]
    half_a = ATT_HEAD_DIM // ROT_FRACTION // 2
    half_i = IDX_DIM // ROT_FRACTION // 2

    q = _rope(za[:, A_Q:A_Q + ATT_WIDTH], cos_a, sin_a, half_a, ATT_HEAD_DIM)
    q_ref[0] = (q * (ATT_HEAD_DIM ** -0.5)).astype(BF16)
    qi = _rope(za[:, A_QI:A_QI + IDX_WIDTH], cos_i, sin_i, half_i, IDX_DIM)
    qi_ref[0] = qi.astype(BF16)
    k = _rope(za[:, A_K:A_K + LANES], cos_a, sin_a, half_a, ATT_HEAD_DIM)
    k_ref[0] = k[:, :ATT_HEAD_DIM].astype(BF16)
    v_ref[0] = za[:, A_V:A_V + ATT_HEAD_DIM].astype(BF16)
    ki = _rope(za[:, A_KI:A_KI + LANES], cos_i, sin_i, half_i, IDX_DIM)
    ki_ref[0] = ki[:, :IDX_DIM].astype(BF16)
    wi_ref[0] = za[:, A_WI:A_WI + IDX_HEADS] * ((IDX_HEADS * IDX_DIM) ** -0.5)

    zb = _dot(xn, wb_ref[...])
    gate_b = zb[:, 0:SC_WIDTH]
    cu = zb[:, SC_WIDTH:2 * SC_WIDTH] * zb[:, 2 * SC_WIDTH:3 * SC_WIDTH]

    @pl.when(s == 0)
    def _():
        cbuf_ref[0:SC_HALO, :] = jnp.zeros((SC_HALO, SC_WIDTH), F32)

    @pl.when(s > 0)
    def _():
        cbuf_ref[0:SC_HALO, :] = carry_ref[b]

    cbuf_ref[SC_HALO:SC_HALO + tm, :] = cu
    carry_ref[b] = cu[tm - SC_HALO:tm, :]
    conv = cu * cw_ref[SC_KERNEL - 1:SC_KERNEL, :]
    for j in range(SC_KERNEL - 1):
        off = SC_HALO - (SC_KERNEL - 1) + j
        conv = conv + cbuf_ref[off:off + tm, :] * cw_ref[j:j + 1, :]
    yc_ref[0] = (gate_b * conv).astype(BF16)


def _mix_in(h, g, wa, wb, conv_w, tab, tm=512):
    bsz, seq, _ = h.shape
    tm = min(tm, seq)
    tok = lambda w: pl.BlockSpec((1, tm, w), lambda s, b: (b, s, 0))
    out_widths = (ATT_WIDTH, IDX_WIDTH, ATT_HEAD_DIM, ATT_HEAD_DIM, IDX_DIM, IDX_HEADS, SC_WIDTH)
    out_dtypes = (BF16, BF16, BF16, BF16, BF16, F32, BF16)
    return pl.pallas_call(
        _mix_in_kernel, grid=(seq // tm, bsz),
        in_specs=[tok(D_MODEL), _const_spec(g.shape), _const_spec(wa.shape), _const_spec(wb.shape),
                  _const_spec(conv_w.shape), pl.BlockSpec((tm, 4 * LANES), lambda s, b: (s, 0))],
        out_specs=[tok(w) for w in out_widths],
        out_shape=[jax.ShapeDtypeStruct((bsz, seq, w), dt) for w, dt in zip(out_widths, out_dtypes)],
        scratch_shapes=[pltpu.VMEM((tm + SC_HALO, SC_WIDTH), F32),
                        pltpu.VMEM((bsz, SC_HALO, SC_WIDTH), F32)],
        compiler_params=pltpu.CompilerParams(dimension_semantics=("arbitrary", "arbitrary"),
                                             vmem_limit_bytes=VMEM_LIMIT),
        name="mix_in",
    )(h, g, wa, wb, conv_w, tab)


def _sortable(x):
    bits = pltpu.bitcast(x + 0.0, I32)
    return bits ^ ((bits >> 31) & 0x7FFFFFFF)


def _dsa_kernel(q_ref, qi_ref, wi_ref, k_ref, v_ref, ki_ref, o_ref,
                qs_ref, qis_ref, keys_ref, thr_ref, m_ref, l_ref, acc_ref, *, topk):
    i = pl.program_id(1)
    kc = KEY_CHUNK
    n_sub = kc // LANES
    n_chunks = (i * Q_BLOCK + Q_BLOCK + kc - 1) // kc
    q_pos = i * Q_BLOCK + lax.broadcasted_iota(I32, (Q_BLOCK, kc), 0)

    for h in range(ATT_HEADS):
        qs_ref[h * Q_BLOCK:(h + 1) * Q_BLOCK, :] = q_ref[0, :, h * ATT_HEAD_DIM:(h + 1) * ATT_HEAD_DIM]
    for h in range(IDX_HEADS):
        qis_ref[h * Q_BLOCK:(h + 1) * Q_BLOCK, :] = qi_ref[0, :, h * IDX_DIM:(h + 1) * IDX_DIM]

    wi = wi_ref[0]

    def score_chunk(c, carry):
        start = pl.multiple_of(c * kc, kc)
        dots = _dot_nt(qis_ref[...], ki_ref[0, pl.ds(start, kc), :])
        score = jnp.zeros((Q_BLOCK, kc), F32)
        for h in range(IDX_HEADS):
            score = score + wi[:, h:h + 1] * jnp.maximum(dots[h * Q_BLOCK:(h + 1) * Q_BLOCK, :], 0.0)
        k_pos = start + lax.broadcasted_iota(I32, (Q_BLOCK, kc), 1)
        score = jnp.where(k_pos <= q_pos, score, -jnp.inf)
        keys_ref[c] = _sortable(score)
        return carry

    lax.fori_loop(0, n_chunks, score_chunk, 0)

    def count_ge(cand):
        def body(c, acc):
            kk = keys_ref[c]
            for j in range(n_sub):
                acc = acc + jnp.where(kk[:, j * LANES:(j + 1) * LANES] >= cand, 1.0, 0.0)
            return acc
        acc = lax.fori_loop(0, n_chunks, body, jnp.zeros((Q_BLOCK, LANES), F32))
        return jnp.sum(acc, axis=1, keepdims=True)

    neg_inf_key = jnp.int32(-2 ** 31 + 0x007FFFFF)

    @pl.when((i + 1) * Q_BLOCK <= topk)
    def _():
        thr_ref[...] = jnp.full((Q_BLOCK, 1), neg_inf_key + 1, I32)

    @pl.when((i + 1) * Q_BLOCK > topk)
    def _():
        kf = jnp.float32(topk)
        cnt0 = count_ge(jnp.zeros((Q_BLOCK, 1), I32))
        take0 = cnt0 >= kf
        thr = jnp.where(take0, 0, INT_MIN).astype(I32)
        cnt = jnp.where(take0, cnt0, (n_chunks * kc).astype(F32))

        def bit_step(t, carry):
            thr, cnt = carry
            cand = thr | (jnp.int32(1) << (30 - t))
            c = count_ge(cand)
            take = c >= kf
            return jnp.where(take, cand, thr), jnp.where(take, c, cnt)

        thr, cnt = lax.fori_loop(0, 31, bit_step, (thr, cnt))
        thr_ref[...] = thr

        @pl.when(jnp.max(cnt) > kf)
        def _():
            need = kf - count_ge(thr + 1)
            r_i = lax.broadcasted_iota(I32, (kc, kc), 0)
            c_i = lax.broadcasted_iota(I32, (kc, kc), 1)
            tri = jnp.where(r_i <= c_i, 1.0, 0.0).astype(BF16)

            def tie_chunk(c, seen):
                kk = keys_ref[c]
                eq = kk == thr
                rank = seen + _dot(jnp.where(eq, 1.0, 0.0).astype(BF16), tri)
                keys_ref[c] = jnp.where(eq & (rank > need), INT_MIN, kk)
                return rank[:, kc - 1:kc]

            lax.fori_loop(0, n_chunks, tie_chunk, jnp.zeros((Q_BLOCK, 1), F32))

    m_ref[...] = jnp.full(m_ref.shape, NEG_BIG, F32)
    l_ref[...] = jnp.zeros(l_ref.shape, F32)
    acc_ref[...] = jnp.zeros(acc_ref.shape, F32)
    thr = thr_ref[...]

    def attn_chunk(c, carry):
        start = pl.multiple_of(c * kc, kc)
        sel = keys_ref[c] >= thr
        logits = _dot_nt(qs_ref[...], k_ref[0, pl.ds(start, kc), :])
        v_c = v_ref[0, pl.ds(start, kc), :]
        for h in range(ATT_HEADS):
            rows = slice(h * Q_BLOCK, (h + 1) * Q_BLOCK)
            lg = jnp.where(sel, logits[rows, :], NEG_BIG)
            m_old = m_ref[rows, :]
            m_new = jnp.maximum(m_old, jnp.max(lg, axis=1, keepdims=True))
            p = jnp.exp(lg - m_new)
            alpha = jnp.exp(m_old - m_new)
            l_ref[rows, :] = alpha * l_ref[rows, :] + jnp.sum(p, axis=1, keepdims=True)
            acc_ref[rows, :] = alpha * acc_ref[rows, :] + _dot(p.astype(BF16), v_c)
            m_ref[rows, :] = m_new
        return carry

    lax.fori_loop(0, n_chunks, attn_chunk, 0)

    for h in range(ATT_HEADS):
        rows = slice(h * Q_BLOCK, (h + 1) * Q_BLOCK)
        o_ref[0, :, h * ATT_HEAD_DIM:(h + 1) * ATT_HEAD_DIM] = (acc_ref[rows, :] / l_ref[rows, :]).astype(BF16)


def _dsa(q, qi, wi, k, v, ki):
    bsz, seq, _ = q.shape
    topk = min(TOPK_MAX, seq // 4)
    n_blk = seq // Q_BLOCK
    blk = lambda w: pl.BlockSpec((1, Q_BLOCK, w), lambda b, i: (b, i, 0))
    full = lambda w: pl.BlockSpec((1, seq, w), lambda b, i: (b, 0, 0))
    rows = ATT_HEADS * Q_BLOCK
    return pl.pallas_call(
        functools.partial(_dsa_kernel, topk=topk), grid=(bsz, n_blk),
        in_specs=[blk(ATT_WIDTH), blk(IDX_WIDTH), blk(IDX_HEADS),
                  full(ATT_HEAD_DIM), full(ATT_HEAD_DIM), full(IDX_DIM)],
        out_specs=blk(ATT_WIDTH),
        out_shape=jax.ShapeDtypeStruct((bsz, seq, ATT_WIDTH), BF16),
        scratch_shapes=[pltpu.VMEM((rows, ATT_HEAD_DIM), BF16),
                        pltpu.VMEM((IDX_HEADS * Q_BLOCK, IDX_DIM), BF16),
                        pltpu.VMEM((seq // KEY_CHUNK, Q_BLOCK, KEY_CHUNK), I32),
                        pltpu.VMEM((Q_BLOCK, 1), I32),
                        pltpu.VMEM((rows, 1), F32),
                        pltpu.VMEM((rows, 1), F32),
                        pltpu.VMEM((rows, ATT_HEAD_DIM), F32)],
        compiler_params=pltpu.CompilerParams(dimension_semantics=("arbitrary", "arbitrary"),
                                             vmem_limit_bytes=VMEM_LIMIT),
        name="dsa",
    )(q, qi, wi, k, v, ki)


def _conf_in_kernel(x_ref, g_ref, w1_ref, b1_ref, cw_ref, cb_ref, lg_ref, lb_ref, o_ref, ubuf_ref):
    s = pl.program_id(1)
    tm = x_ref.shape[1]
    xn = _rms(x_ref[0], g_ref[...]).astype(BF16)
    z = _dot(xn, w1_ref[...]) + b1_ref[...]
    u = z[:, :CONF_WIDTH] * _sigmoid(z[:, CONF_WIDTH:])

    @pl.when(s == 0)
    def _():
        ubuf_ref[0:CONF_HALO, :] = jnp.zeros((CONF_HALO, CONF_WIDTH), F32)

    @pl.when(s > 0)
    def _():
        ubuf_ref[0:CONF_HALO, :] = ubuf_ref[tm:tm + CONF_HALO, :]

    ubuf_ref[CONF_HALO:CONF_HALO + tm, :] = u
    conv = u * cw_ref[CONF_KERNEL - 1:CONF_KERNEL, :] + cb_ref[...]
    for j in range(CONF_KERNEL - 1):
        off = CONF_HALO - (CONF_KERNEL - 1) + j
        conv = conv + ubuf_ref[off:off + tm, :] * cw_ref[j:j + 1, :]
    mu = jnp.mean(conv, axis=-1, keepdims=True)
    xc = conv - mu
    var = jnp.mean(xc * xc, axis=-1, keepdims=True)
    y = xc * lax.rsqrt(var + NORM_EPS) * lg_ref[...] + lb_ref[...]
    o_ref[0] = (y * _sigmoid(y)).astype(BF16)


def _conf_in(h, g, w1, b1, conv_w, conv_b, ln_g, ln_b, tm=256):
    bsz, seq, _ = h.shape
    tm = min(tm, seq)
    tok = lambda w: pl.BlockSpec((1, tm, w), lambda b, s: (b, s, 0))
    consts = (g, w1, b1, conv_w, conv_b, ln_g, ln_b)
    return pl.pallas_call(
        _conf_in_kernel, grid=(bsz, seq // tm),
        in_specs=[tok(D_MODEL)] + [_const_spec(c.shape) for c in consts],
        out_specs=tok(CONF_WIDTH),
        out_shape=jax.ShapeDtypeStruct((bsz, seq, CONF_WIDTH), BF16),
        scratch_shapes=[pltpu.VMEM((tm + CONF_HALO, CONF_WIDTH), F32)],
        compiler_params=pltpu.CompilerParams(dimension_semantics=("arbitrary", "arbitrary"),
                                             vmem_limit_bytes=VMEM_LIMIT),
        name="conf_in",
    )(h, *consts)


def _rope_table(seq, head_dim):
    half = head_dim // ROT_FRACTION // 2
    inv_freq = ROPE_THETA ** (-jnp.arange(half, dtype=F32) / half)
    ang = jnp.arange(seq, dtype=F32)[:, None] * inv_freq[None, :]
    cos, sin = jnp.cos(ang), jnp.sin(ang)
    rest = head_dim - 2 * half
    cos_h = jnp.concatenate([cos, cos, jnp.ones((seq, rest), F32)], axis=1)
    sin_h = jnp.concatenate([-sin, sin, jnp.zeros((seq, rest), F32)], axis=1)
    reps = LANES // head_dim
    return jnp.tile(cos_h, (1, reps)), jnp.tile(sin_h, (1, reps))


def _pack_w_in(w_in):
    sizes = (ATT_WIDTH, ATT_HEAD_DIM, ATT_HEAD_DIM, IDX_WIDTH, IDX_DIM, IDX_HEADS)
    offs = [0]
    for sz in sizes:
        offs.append(offs[-1] + sz)
    q, k, v, qi, ki, wi = (w_in[:, offs[j]:offs[j + 1]] for j in range(len(sizes)))
    pad = lambda w: jnp.pad(w, ((0, 0), (0, LANES - w.shape[1])))
    wa = jnp.concatenate([q, qi, pad(k), pad(v), pad(ki), pad(wi)], axis=1)
    return wa.astype(BF16), w_in[:, offs[-1]:].astype(BF16)


def kernel(x, ffn_norm, ffn_w_gate, ffn_w_up, ffn_w_down, mix_norm, hyb_w_in, hyb_conv_w, hyb_w_out,
           conf_w_pw1, conf_b_pw1, conf_conv_w, conf_conv_b, conf_ln_g, conf_ln_b, conf_w_pw2,
           conf_b_pw2, final_norm):
    bsz, seq, _ = x.shape
    n = bsz * seq
    depth = ffn_norm.shape[0]
    row = lambda a: a.reshape(1, -1)
    ffn_w = lambda layer, j: (row(ffn_norm[layer, j]), ffn_w_gate[layer, j].astype(BF16),
                              ffn_w_up[layer, j].astype(BF16), ffn_w_down[layer, j].astype(BF16))
    cos_a, sin_a = _rope_table(seq, ATT_HEAD_DIM)
    cos_i, sin_i = _rope_table(seq, IDX_DIM)
    tab = jnp.concatenate([cos_a, sin_a, cos_i, sin_i], axis=1)

    h = x.reshape(n, D_MODEL)
    for layer in range(depth):
        last = layer == depth - 1
        h = _ffn(h, *ffn_w(layer, 0))
        h3 = h.reshape(bsz, seq, D_MODEL)
        if layer % 2 == 0:
            e = layer // 2
            wa, wb = _pack_w_in(hyb_w_in[e])
            q, qi, k, v, ki, wi, y_conv = _mix_in(h3, row(mix_norm[layer]), wa, wb, hyb_conv_w[e], tab)
            y_attn = _dsa(q, qi, wi, k, v, ki)
            w_out = hyb_w_out[e].astype(BF16)
            pre = ((y_attn.reshape(n, ATT_WIDTH), w_out[:ATT_WIDTH]),
                   (y_conv.reshape(n, SC_WIDTH), w_out[ATT_WIDTH:]))
            bias = None
        else:
            o = layer // 2
            u = _conf_in(h3, row(mix_norm[layer]), conf_w_pw1[o].astype(BF16), row(conf_b_pw1[o]),
                         conf_conv_w[o], row(conf_conv_b[o]), row(conf_ln_g[o]), row(conf_ln_b[o]))
            pre = ((u.reshape(n, CONF_WIDTH), conf_w_pw2[o].astype(BF16)),)
            bias = row(conf_b_pw2[o])
        h = _ffn(h, *ffn_w(layer, 1), pre=pre, bias=bias,
                 final_g=row(final_norm) if last else None)
    return h.reshape(bsz, seq, D_MODEL)
```

```python
import functools

import jax
import jax.numpy as jnp
from jax import lax
from jax.experimental import pallas as pl
from jax.experimental.pallas import tpu as pltpu

F32 = jnp.float32
BF16 = jnp.bfloat16
I32 = jnp.int32
I16 = jnp.int16

D_MODEL = 1024
D_FF = 2816
FFN_RES_WEIGHT = 0.5
ATT_HEADS = 8
ATT_HEAD_DIM = 64
ATT_WIDTH = ATT_HEADS * ATT_HEAD_DIM
IDX_HEADS = 8
IDX_DIM = 32
IDX_WIDTH = IDX_HEADS * IDX_DIM
TOPK_MAX = 256
Q_BLOCK = 128
SC_WIDTH = D_MODEL - ATT_WIDTH
SC_KERNEL = 3
CONF_WIDTH = D_MODEL
CONF_KERNEL = 31
ROPE_THETA = 500000.0
ROT_FRACTION = 4
NORM_EPS = 1e-6

LANES = 128
SUBLANES = 8
PACK_ROWS = 16
VMEM_LIMIT = 56 * 1024 * 1024

A_Q = 0
A_QI = A_Q + ATT_WIDTH
A_K = A_QI + IDX_WIDTH
A_KI = A_K + LANES
A_COLS = A_KI + LANES

KEY_CHUNK = 512
VT_ROWS = ATT_HEAD_DIM + IDX_HEADS
QS_COLS = 2 * LANES
GROUP_HEADS = 2
GROUP_ROWS = GROUP_HEADS * Q_BLOCK
NEG_BIG = -1e30
HALF_BIAS = 1 << 15
NEG_INF_KEY = -2 ** 31 + 0x007FFFFF
CONF_HALO = 32
SC_HALO = 8


def _dot(a, b):
    return jnp.dot(a, b, preferred_element_type=F32)


def _dot_nt(a, b):
    return lax.dot_general(a, b, (((1,), (1,)), ((), ())), preferred_element_type=F32)


def _rms(x, g):
    return x * lax.rsqrt(jnp.mean(x * x, axis=-1, keepdims=True) + NORM_EPS) * g


def _sigmoid(x):
    return 1.0 / (1.0 + jnp.exp(-x))


def _ffn_kernel(*refs, n_pre, has_bias, final_norm):
    it = iter(refs)
    x_ref = next(it)
    pre = [(next(it), next(it)) for _ in range(n_pre)]
    bias_ref = next(it) if has_bias else None
    g_ref, wg_ref, wu_ref, wd_ref = next(it), next(it), next(it), next(it)
    gf_ref = next(it) if final_norm else None
    o_ref = next(it)

    x = x_ref[...]
    if n_pre:
        upd = _dot(pre[0][0][...], pre[0][1][...])
        for a_ref, w_ref in pre[1:]:
            upd = upd + _dot(a_ref[...], w_ref[...])
        if has_bias:
            upd = upd + bias_ref[...]
        x = x + upd
    xn = _rms(x, g_ref[...]).astype(BF16)
    gate = _dot(xn, wg_ref[...])
    up = _dot(xn, wu_ref[...])
    mid = (gate * _sigmoid(gate) * up).astype(BF16)
    y = x + FFN_RES_WEIGHT * _dot(mid, wd_ref[...])
    if final_norm:
        y = _rms(y, gf_ref[...])
    o_ref[...] = y


def _const_spec(shape):
    nd = len(shape)
    return pl.BlockSpec(shape, lambda *_: (0,) * nd, pipeline_mode=pl.Buffered(1))


def _ffn(x2d, g, wg, wu, wd, pre=(), bias=None, final_g=None, tm=512):
    n = x2d.shape[0]
    tm = min(tm, n)
    row = lambda w: pl.BlockSpec((tm, w), lambda i: (i, 0))
    args, specs = [x2d], [row(D_MODEL)]
    for act, w in pre:
        args += [act, w]
        specs += [row(act.shape[1]), _const_spec(w.shape)]
    if bias is not None:
        args.append(bias)
        specs.append(_const_spec(bias.shape))
    args += [g, wg, wu, wd]
    specs += [_const_spec(g.shape), _const_spec(wg.shape), _const_spec(wu.shape), _const_spec(wd.shape)]
    if final_g is not None:
        args.append(final_g)
        specs.append(_const_spec(final_g.shape))
    kern = functools.partial(_ffn_kernel, n_pre=len(pre), has_bias=bias is not None,
                             final_norm=final_g is not None)
    return pl.pallas_call(
        kern, grid=(n // tm,), in_specs=specs, out_specs=row(D_MODEL),
        out_shape=jax.ShapeDtypeStruct((n, D_MODEL), F32),
        compiler_params=pltpu.CompilerParams(dimension_semantics=("arbitrary",),
                                             vmem_limit_bytes=VMEM_LIMIT),
        name="ffn",
    )(*args)


def _rope(z, cos, sin, half, period):
    width = z.shape[-1]
    lane = lax.broadcasted_iota(I32, z.shape, 1)
    first = (lane & (period - 1)) < half
    partner = jnp.where(first, pltpu.roll(z, width - half, 1), pltpu.roll(z, half, 1))
    reps = width // LANES
    if reps > 1:
        cos = jnp.concatenate([cos] * reps, axis=1)
        sin = jnp.concatenate([sin] * reps, axis=1)
    return z * cos + partner * sin


def _mix_in_kernel(x_ref, g_ref, wa_ref, wt_ref, wb_ref, cw_ref, tab_ref,
                   q_ref, qi_ref, k_ref, vt_ref, ki_ref, wit_ref, yc_ref,
                   cbuf_ref, carry_ref):
    s = pl.program_id(0)
    b = pl.program_id(1)
    tm = x_ref.shape[1]
    xn = _rms(x_ref[0], g_ref[...]).astype(BF16)
    za = _dot(xn, wa_ref[...])
    cos_a, sin_a = tab_ref[:, 0:LANES], tab_ref[:, LANES:2 * LANES]
    cos_i, sin_i = tab_ref[:, 2 * LANES:3 * LANES], tab_ref[:, 3 * LANES:4 * LANES]
    half_a = ATT_HEAD_DIM // ROT_FRACTION // 2
    half_i = IDX_DIM // ROT_FRACTION // 2

    q = _rope(za[:, A_Q:A_Q + ATT_WIDTH], cos_a, sin_a, half_a, ATT_HEAD_DIM)
    q_ref[0] = (q * (ATT_HEAD_DIM ** -0.5)).astype(BF16)
    qi = _rope(za[:, A_QI:A_QI + IDX_WIDTH], cos_i, sin_i, half_i, IDX_DIM)
    qi_ref[0] = qi.astype(BF16)
    k = _rope(za[:, A_K:A_K + LANES], cos_a, sin_a, half_a, ATT_HEAD_DIM)
    k_ref[0] = k.astype(BF16)
    ki = _rope(za[:, A_KI:A_KI + LANES], cos_i, sin_i, half_i, IDX_DIM)
    ki_ref[0] = ki[:, :IDX_DIM].astype(BF16)
    zt = _dot_nt(wt_ref[...], xn)
    t_row = lax.broadcasted_iota(I32, zt.shape, 0)
    vt_ref[0, 0] = jnp.where(t_row < ATT_HEAD_DIM, zt,
                             jnp.where(t_row == ATT_HEAD_DIM, 1.0, 0.0)).astype(BF16)
    wit_ref[0] = zt[ATT_HEAD_DIM:, :] * ((IDX_HEADS * IDX_DIM) ** -0.5)

    zb = _dot(xn, wb_ref[...])
    gate_b = zb[:, 0:SC_WIDTH]
    cu = zb[:, SC_WIDTH:2 * SC_WIDTH] * zb[:, 2 * SC_WIDTH:3 * SC_WIDTH]

    @pl.when(s == 0)
    def _():
        cbuf_ref[0:SC_HALO, :] = jnp.zeros((SC_HALO, SC_WIDTH), F32)

    @pl.when(s > 0)
    def _():
        cbuf_ref[0:SC_HALO, :] = carry_ref[b]

    cbuf_ref[SC_HALO:SC_HALO + tm, :] = cu
    carry_ref[b] = cu[tm - SC_HALO:tm, :]
    conv = cu * cw_ref[SC_KERNEL - 1:SC_KERNEL, :]
    for j in range(SC_KERNEL - 1):
        off = SC_HALO - (SC_KERNEL - 1) + j
        conv = conv + cbuf_ref[off:off + tm, :] * cw_ref[j:j + 1, :]
    yc_ref[0] = (gate_b * conv).astype(BF16)


def _mix_in(h, g, wa, wt, wb, conv_w, tab):
    bsz, seq, _ = h.shape
    tm = KEY_CHUNK
    tok = lambda w: pl.BlockSpec((1, tm, w), lambda s, b: (b, s, 0))
    sds = jax.ShapeDtypeStruct
    out_shape = [sds((bsz, seq, ATT_WIDTH), BF16), sds((bsz, seq, IDX_WIDTH), BF16),
                 sds((bsz, seq, LANES), BF16), sds((bsz, seq // tm, VT_ROWS, tm), BF16),
                 sds((bsz, seq, IDX_DIM), BF16), sds((bsz, IDX_HEADS, seq), F32),
                 sds((bsz, seq, SC_WIDTH), BF16)]
    out_specs = [tok(ATT_WIDTH), tok(IDX_WIDTH), tok(LANES),
                 pl.BlockSpec((1, 1, VT_ROWS, tm), lambda s, b: (b, s, 0, 0)),
                 tok(IDX_DIM), pl.BlockSpec((1, IDX_HEADS, tm), lambda s, b: (b, 0, s)),
                 tok(SC_WIDTH)]
    return pl.pallas_call(
        _mix_in_kernel, grid=(seq // tm, bsz),
        in_specs=[tok(D_MODEL), _const_spec(g.shape), _const_spec(wa.shape), _const_spec(wt.shape),
                  _const_spec(wb.shape), _const_spec(conv_w.shape),
                  pl.BlockSpec((tm, 4 * LANES), lambda s, b: (s, 0))],
        out_specs=out_specs, out_shape=out_shape,
        scratch_shapes=[pltpu.VMEM((tm + SC_HALO, SC_WIDTH), F32),
                        pltpu.VMEM((bsz, SC_HALO, SC_WIDTH), F32)],
        compiler_params=pltpu.CompilerParams(dimension_semantics=("arbitrary", "arbitrary"),
                                             vmem_limit_bytes=VMEM_LIMIT),
        name="mix_in",
    )(h, g, wa, wt, wb, conv_w, tab)


def _sortable(x):
    bits = pltpu.bitcast(x + 0.0, I32)
    return bits ^ ((bits >> 31) & 0x7FFFFFFF)


def _count_ge(arr_ref, n_chunks, cand):
    kc = arr_ref.shape[1]
    c16 = cand.astype(I16)
    one, zero = jnp.int16(1), jnp.int16(0)
    n_acc = 4

    def body(c, accs):
        accs = list(accs)
        for r in range(kc // PACK_ROWS):
            t = arr_ref[c, r * PACK_ROWS:(r + 1) * PACK_ROWS, :]
            accs[r % n_acc] = accs[r % n_acc] + jnp.where(t >= c16, one, zero)
        return tuple(accs)

    accs = lax.fori_loop(0, n_chunks, body,
                         tuple(jnp.zeros((PACK_ROWS, Q_BLOCK), I16) for _ in range(n_acc)))
    total = (accs[0] + accs[1]) + (accs[2] + accs[3])
    return jnp.sum(total.astype(F32), axis=0, keepdims=True)


def _search16(arr_ref, n_chunks, base, cge0, cgt0, kf):
    def step(t, carry):
        thr, cge, cgt = carry
        cand = thr | (jnp.int32(1) << (15 - t))
        n = base + _count_ge(arr_ref, n_chunks, cand - HALF_BIAS)
        take = n >= kf
        return jnp.where(take, cand, thr), jnp.where(take, n, cge), jnp.where(take, cgt, n)

    return lax.fori_loop(0, 16, step, (jnp.zeros((1, Q_BLOCK), I32), cge0, cgt0))


def _dsa_kernel(q_ref, qi_ref, wit_ref, k_ref, vt_ref, ki_ref, o_ref,
                qs_ref, qis_ref, hi_ref, lo_ref, low_ref, bias_ref, thr_ref, need_ref, acc_ref,
                lga_ref, lgb_ref, *, topk):
    i = pl.program_id(1)
    kc = KEY_CHUNK
    rows = ATT_HEADS * Q_BLOCK
    n_chunks = (i * Q_BLOCK + Q_BLOCK + kc - 1) // kc
    n_pairs = (n_chunks + 1) // 2
    q_pos = i * Q_BLOCK + lax.broadcasted_iota(I32, (kc, Q_BLOCK), 1)
    kf = jnp.float32(topk)

    r_i = lax.broadcasted_iota(I32, (Q_BLOCK, Q_BLOCK), 0)
    c_i = lax.broadcasted_iota(I32, (Q_BLOCK, Q_BLOCK), 1)
    eye = jnp.where(r_i == c_i, 1.0, 0.0).astype(BF16)
    for h in range(ATT_HEADS):
        blk_rows = slice(h * Q_BLOCK, (h + 1) * Q_BLOCK)
        qs_ref[blk_rows, 0:LANES] = jnp.zeros((Q_BLOCK, LANES), BF16)
        qs_ref[blk_rows, 0:ATT_HEAD_DIM] = q_ref[0, :, h * ATT_HEAD_DIM:(h + 1) * ATT_HEAD_DIM]
        qs_ref[blk_rows, LANES:QS_COLS] = eye
    for h in range(IDX_HEADS):
        qis_ref[h * Q_BLOCK:(h + 1) * Q_BLOCK, :] = qi_ref[0, :, h * IDX_DIM:(h + 1) * IDX_DIM]

    wit = wit_ref[0]

    def score_chunk(c):
        start = pl.multiple_of(c * kc, kc)
        ki_c = ki_ref[0, pl.ds(start, kc), :]
        score = jnp.zeros((kc, Q_BLOCK), F32)
        for g in range(IDX_HEADS // GROUP_HEADS):
            dots = _dot_nt(ki_c, qis_ref[g * GROUP_ROWS:(g + 1) * GROUP_ROWS, :])
            for j in range(GROUP_HEADS):
                h = g * GROUP_HEADS + j
                score = score + wit[h:h + 1, :] * jnp.maximum(dots[:, j * Q_BLOCK:(j + 1) * Q_BLOCK], 0.0)
        k_pos = start + lax.broadcasted_iota(I32, (kc, Q_BLOCK), 0)
        key = _sortable(jnp.where(k_pos <= q_pos, score, -jnp.inf))
        hi_ref[c] = (key >> 16).astype(I16)
        lo_ref[c] = ((key & 0xFFFF) - HALF_BIAS).astype(I16)

    def score_pair(p, carry):
        score_chunk(2 * p)
        score_chunk(2 * p + 1)
        return carry

    lax.fori_loop(0, n_pairs, score_pair, 0)

    @pl.when((i + 1) * Q_BLOCK <= topk)
    def _():
        above = NEG_INF_KEY + 1
        thr_ref[0] = jnp.full((1, Q_BLOCK), above >> 16, I32)
        thr_ref[1] = jnp.full((1, Q_BLOCK), (above & 0xFFFF) - HALF_BIAS, I32)
        need_ref[0] = jnp.zeros((1, Q_BLOCK), F32)
        need_ref[1] = jnp.zeros((1, Q_BLOCK), F32)

    @pl.when((i + 1) * Q_BLOCK > topk)
    def _():
        total = jnp.full((1, Q_BLOCK), n_chunks * kc, I32).astype(F32)
        zero = jnp.zeros((1, Q_BLOCK), F32)
        t_hi, cge_hi, cgt_hi = _search16(hi_ref, n_chunks, zero, total, zero, kf)
        t_hi = t_hi - HALF_BIAS
        t_hi16 = t_hi.astype(I16)

        def bucket_chunk(c, carry):
            low_ref[c] = jnp.where(hi_ref[c] == t_hi16, lo_ref[c], jnp.int16(-HALF_BIAS))
            return carry

        lax.fori_loop(0, n_chunks, bucket_chunk, 0)
        t_lo, cge, cgt = _search16(low_ref, n_chunks, cgt_hi, cge_hi, cgt_hi, kf)
        thr_ref[0] = t_hi
        thr_ref[1] = t_lo - HALF_BIAS
        need_ref[0] = kf - cgt
        need_ref[1] = cge - cgt

    t_hi16 = thr_ref[0].astype(I16)
    t_lo16 = thr_ref[1].astype(I16)

    def bias_chunk(c, carry):
        hi, lo = hi_ref[c], lo_ref[c]
        sel = (hi > t_hi16) | ((hi == t_hi16) & (lo >= t_lo16))
        bias_ref[c] = jnp.where(sel, jnp.bfloat16(0.0), jnp.bfloat16(NEG_BIG))
        return carry

    lax.fori_loop(0, 2 * n_pairs, bias_chunk, 0)

    @pl.when(jnp.max(need_ref[1] - need_ref[0]) > 0.0)
    def _():
        need = need_ref[0]
        r_i = lax.broadcasted_iota(I32, (kc, kc), 0)
        c_i = lax.broadcasted_iota(I32, (kc, kc), 1)
        tri = jnp.where(c_i <= r_i, 1.0, 0.0).astype(BF16)

        def tie_chunk(c, seen):
            eq = (hi_ref[c] == t_hi16) & (lo_ref[c] == t_lo16)
            eq_b = jnp.where(eq, jnp.bfloat16(1.0), jnp.bfloat16(0.0))
            rank = seen + _dot(tri, eq_b)
            drop = (eq_b.astype(F32) > 0.5) & (rank > need)
            bias_ref[c] = jnp.where(drop, NEG_BIG, bias_ref[c].astype(F32)).astype(BF16)
            return rank[kc - 1:kc, :]

        lax.fori_loop(0, n_chunks, tie_chunk, jnp.zeros((1, Q_BLOCK), F32))

    acc_ref[...] = jnp.zeros(acc_ref.shape, F32)

    n_groups = ATT_HEADS // GROUP_HEADS
    group = lambda g: slice(g * GROUP_ROWS, (g + 1) * GROUP_ROWS)

    def keys_of(c):
        start = pl.multiple_of(c * kc, kc)
        return jnp.concatenate([k_ref[0, pl.ds(start, kc), :], bias_ref[c]], axis=1)

    def softmax_step(lg_ref, g, v_c, m):
        lg = lg_ref[:, group(g)]
        m_new = jnp.maximum(m, jnp.max(lg, axis=0, keepdims=True))
        p = jnp.exp(lg - m_new)
        acc_ref[:, group(g)] = jnp.exp(m - m_new) * acc_ref[:, group(g)] + _dot(v_c, p.astype(BF16))
        return m_new

    last = 2 * n_pairs - 1
    lga_ref[...] = _dot_nt(keys_of(0), qs_ref[...])

    def half_step(src_ref, dst_ref, c_src, c_dst, ms):
        keys, v_c = keys_of(c_dst), vt_ref[0, c_src]
        out = []
        for g in range(n_groups):
            dst_ref[:, group(g)] = _dot_nt(keys, qs_ref[group(g), :])
            out.append(softmax_step(src_ref, g, v_c, ms[g]))
        return tuple(out)

    def attn_pair(p, ms):
        ms = half_step(lga_ref, lgb_ref, 2 * p, 2 * p + 1, ms)
        return half_step(lgb_ref, lga_ref, 2 * p + 1, jnp.minimum(2 * p + 2, last), ms)

    lax.fori_loop(0, n_pairs, attn_pair,
                  tuple(jnp.full((1, GROUP_ROWS), NEG_BIG, F32) for _ in range(n_groups)))

    denom = acc_ref[ATT_HEAD_DIM:ATT_HEAD_DIM + 1, :]
    out = (acc_ref[0:ATT_HEAD_DIM, :] / denom).astype(BF16)
    for h in range(ATT_HEADS):
        o_ref[0, :, h * ATT_HEAD_DIM:(h + 1) * ATT_HEAD_DIM] = _dot_nt(
            eye, out[:, h * Q_BLOCK:(h + 1) * Q_BLOCK]).astype(BF16)


def _dsa(q, qi, wi_t, k, v_t, ki):
    bsz, seq, _ = q.shape
    topk = min(TOPK_MAX, seq // 4)
    assert topk % Q_BLOCK == 0 and seq % KEY_CHUNK == 0
    n_blk = seq // Q_BLOCK
    n_ck = seq // KEY_CHUNK
    blk = lambda w: pl.BlockSpec((1, Q_BLOCK, w), lambda b, i: (b, i, 0))
    full = lambda w: pl.BlockSpec((1, seq, w), lambda b, i: (b, 0, 0))
    rows = ATT_HEADS * Q_BLOCK
    return pl.pallas_call(
        functools.partial(_dsa_kernel, topk=topk), grid=(bsz, n_blk),
        in_specs=[blk(ATT_WIDTH), blk(IDX_WIDTH),
                  pl.BlockSpec((1, IDX_HEADS, Q_BLOCK), lambda b, i: (b, 0, i)),
                  full(LANES),
                  pl.BlockSpec((1, n_ck, VT_ROWS, KEY_CHUNK), lambda b, i: (b, 0, 0, 0)),
                  full(IDX_DIM)],
        out_specs=blk(ATT_WIDTH),
        out_shape=jax.ShapeDtypeStruct((bsz, seq, ATT_WIDTH), BF16),
        scratch_shapes=[pltpu.VMEM((rows, QS_COLS), BF16),
                        pltpu.VMEM((IDX_HEADS * Q_BLOCK, IDX_DIM), BF16),
                        pltpu.VMEM((n_ck, KEY_CHUNK, Q_BLOCK), I16),
                        pltpu.VMEM((n_ck, KEY_CHUNK, Q_BLOCK), I16),
                        pltpu.VMEM((n_ck, KEY_CHUNK, Q_BLOCK), I16),
                        pltpu.VMEM((n_ck, KEY_CHUNK, Q_BLOCK), BF16),
                        pltpu.VMEM((2, 1, Q_BLOCK), I32),
                        pltpu.VMEM((2, 1, Q_BLOCK), F32),
                        pltpu.VMEM((VT_ROWS, rows), F32),
                        pltpu.VMEM((KEY_CHUNK, rows), F32),
                        pltpu.VMEM((KEY_CHUNK, rows), F32)],
        compiler_params=pltpu.CompilerParams(dimension_semantics=("arbitrary", "arbitrary"),
                                             vmem_limit_bytes=VMEM_LIMIT),
        name="dsa",
    )(q, qi, wi_t, k, v_t, ki)


def _conf_in_kernel(x_ref, g_ref, w1_ref, b1_ref, cw_ref, cb_ref, lg_ref, lb_ref, o_ref, ubuf_ref):
    s = pl.program_id(1)
    tm = x_ref.shape[1]
    xn = _rms(x_ref[0], g_ref[...]).astype(BF16)
    z = _dot(xn, w1_ref[...]) + b1_ref[...]
    u = z[:, :CONF_WIDTH] * _sigmoid(z[:, CONF_WIDTH:])

    @pl.when(s == 0)
    def _():
        ubuf_ref[0:CONF_HALO, :] = jnp.zeros((CONF_HALO, CONF_WIDTH), F32)

    @pl.when(s > 0)
    def _():
        ubuf_ref[0:CONF_HALO, :] = ubuf_ref[tm:tm + CONF_HALO, :]

    ubuf_ref[CONF_HALO:CONF_HALO + tm, :] = u
    conv = u * cw_ref[CONF_KERNEL - 1:CONF_KERNEL, :] + cb_ref[...]
    for j in range(CONF_KERNEL - 1):
        off = CONF_HALO - (CONF_KERNEL - 1) + j
        conv = conv + ubuf_ref[off:off + tm, :] * cw_ref[j:j + 1, :]
    mu = jnp.mean(conv, axis=-1, keepdims=True)
    xc = conv - mu
    var = jnp.mean(xc * xc, axis=-1, keepdims=True)
    y = xc * lax.rsqrt(var + NORM_EPS) * lg_ref[...] + lb_ref[...]
    o_ref[0] = (y * _sigmoid(y)).astype(BF16)


def _conf_in(h, g, w1, b1, conv_w, conv_b, ln_g, ln_b, tm=256):
    bsz, seq, _ = h.shape
    tm = min(tm, seq)
    tok = lambda w: pl.BlockSpec((1, tm, w), lambda b, s: (b, s, 0))
    consts = (g, w1, b1, conv_w, conv_b, ln_g, ln_b)
    return pl.pallas_call(
        _conf_in_kernel, grid=(bsz, seq // tm),
        in_specs=[tok(D_MODEL)] + [_const_spec(c.shape) for c in consts],
        out_specs=tok(CONF_WIDTH),
        out_shape=jax.ShapeDtypeStruct((bsz, seq, CONF_WIDTH), BF16),
        scratch_shapes=[pltpu.VMEM((tm + CONF_HALO, CONF_WIDTH), F32)],
        compiler_params=pltpu.CompilerParams(dimension_semantics=("arbitrary", "arbitrary"),
                                             vmem_limit_bytes=VMEM_LIMIT),
        name="conf_in",
    )(h, *consts)


def _rope_table(seq, head_dim):
    half = head_dim // ROT_FRACTION // 2
    inv_freq = ROPE_THETA ** (-jnp.arange(half, dtype=F32) / half)
    ang = jnp.arange(seq, dtype=F32)[:, None] * inv_freq[None, :]
    cos, sin = jnp.cos(ang), jnp.sin(ang)
    rest = head_dim - 2 * half
    cos_h = jnp.concatenate([cos, cos, jnp.ones((seq, rest), F32)], axis=1)
    sin_h = jnp.concatenate([-sin, sin, jnp.zeros((seq, rest), F32)], axis=1)
    reps = LANES // head_dim
    return jnp.tile(cos_h, (1, reps)), jnp.tile(sin_h, (1, reps))


def _pack_w_in(w_in):
    sizes = (ATT_WIDTH, ATT_HEAD_DIM, ATT_HEAD_DIM, IDX_WIDTH, IDX_DIM, IDX_HEADS)
    offs = [0]
    for sz in sizes:
        offs.append(offs[-1] + sz)
    q, k, v, qi, ki, wi = (w_in[:, offs[j]:offs[j + 1]] for j in range(len(sizes)))
    pad = lambda w: jnp.pad(w, ((0, 0), (0, LANES - w.shape[1])))
    wa = jnp.concatenate([q, qi, pad(k), pad(ki)], axis=1)
    wt = jnp.concatenate([v, wi], axis=1).T
    return wa.astype(BF16), wt.astype(BF16), w_in[:, offs[-1]:].astype(BF16)


def kernel(x, ffn_norm, ffn_w_gate, ffn_w_up, ffn_w_down, mix_norm, hyb_w_in, hyb_conv_w, hyb_w_out,
           conf_w_pw1, conf_b_pw1, conf_conv_w, conf_conv_b, conf_ln_g, conf_ln_b, conf_w_pw2,
           conf_b_pw2, final_norm):
    bsz, seq, _ = x.shape
    n = bsz * seq
    depth = ffn_norm.shape[0]
    row = lambda a: a.reshape(1, -1)
    ffn_w = lambda layer, j: (row(ffn_norm[layer, j]), ffn_w_gate[layer, j].astype(BF16),
                              ffn_w_up[layer, j].astype(BF16), ffn_w_down[layer, j].astype(BF16))
    cos_a, sin_a = _rope_table(seq, ATT_HEAD_DIM)
    cos_i, sin_i = _rope_table(seq, IDX_DIM)
    tab = jnp.concatenate([cos_a, sin_a, cos_i, sin_i], axis=1)

    h = x.reshape(n, D_MODEL)
    for layer in range(depth):
        last = layer == depth - 1
        h = _ffn(h, *ffn_w(layer, 0))
        h3 = h.reshape(bsz, seq, D_MODEL)
        if layer % 2 == 0:
            e = layer // 2
            wa, wt, wb = _pack_w_in(hyb_w_in[e])
            q, qi, k, v_t, ki, wi_t, y_conv = _mix_in(h3, row(mix_norm[layer]), wa, wt, wb,
                                                      hyb_conv_w[e], tab)
            y_attn = _dsa(q, qi, wi_t, k, v_t, ki)
            w_out = hyb_w_out[e].astype(BF16)
            pre = ((y_attn.reshape(n, ATT_WIDTH), w_out[:ATT_WIDTH]),
                   (y_conv.reshape(n, SC_WIDTH), w_out[ATT_WIDTH:]))
            bias = None
        else:
            o = layer // 2
            u = _conf_in(h3, row(mix_norm[layer]), conf_w_pw1[o].astype(BF16), row(conf_b_pw1[o]),
                         conf_conv_w[o], row(conf_conv_b[o]), row(conf_ln_g[o]), row(conf_ln_b[o]))
            pre = ((u.reshape(n, CONF_WIDTH), conf_w_pw2[o].astype(BF16)),)
            bias = row(conf_b_pw2[o])
        h = _ffn(h, *ffn_w(layer, 1), pre=pre, bias=bias,
                 final_g=row(final_norm) if last else None)
    return h.reshape(bsz, seq, D_MODEL)
```

```python
import functools

import jax
import jax.numpy as jnp
from jax import lax
from jax.experimental import pallas as pl
from jax.experimental.pallas import tpu as pltpu

F32 = jnp.float32
BF16 = jnp.bfloat16
I32 = jnp.int32
I16 = jnp.int16

D_MODEL = 1024
D_FF = 2816
FFN_RES_WEIGHT = 0.5
ATT_HEADS = 8
ATT_HEAD_DIM = 64
ATT_WIDTH = ATT_HEADS * ATT_HEAD_DIM
IDX_HEADS = 8
IDX_DIM = 32
IDX_WIDTH = IDX_HEADS * IDX_DIM
TOPK_MAX = 256
Q_BLOCK = 128
SC_WIDTH = D_MODEL - ATT_WIDTH
SC_KERNEL = 3
CONF_WIDTH = D_MODEL
CONF_KERNEL = 31
ROPE_THETA = 500000.0
ROT_FRACTION = 4
NORM_EPS = 1e-6

LANES = 128
SUBLANES = 8
PACK_ROWS = 16
VMEM_LIMIT = 56 * 1024 * 1024

A_Q = 0
A_QI = A_Q + ATT_WIDTH
A_K = A_QI + IDX_WIDTH
A_KI = A_K + LANES
A_COLS = A_KI + LANES

KEY_CHUNK = 512
VT_ROWS = ATT_HEAD_DIM + IDX_HEADS
QS_COLS = 2 * LANES
SEARCH_GROUP = 4
GROUP_HEADS = 2
GROUP_ROWS = GROUP_HEADS * Q_BLOCK
NEG_BIG = -1e30
HALF_BIAS = 1 << 15
NEG_INF_KEY = -2 ** 31 + 0x007FFFFF
CONF_HALO = 32
SC_HALO = 8


def _dot(a, b):
    return jnp.dot(a, b, preferred_element_type=F32)


def _dot_nt(a, b):
    return lax.dot_general(a, b, (((1,), (1,)), ((), ())), preferred_element_type=F32)


def _rms(x, g):
    return x * lax.rsqrt(jnp.mean(x * x, axis=-1, keepdims=True) + NORM_EPS) * g


def _sigmoid(x):
    return 1.0 / (1.0 + jnp.exp(-x))


def _ffn_kernel(*refs, n_pre, has_bias, final_norm):
    it = iter(refs)
    x_ref = next(it)
    pre = [(next(it), next(it)) for _ in range(n_pre)]
    bias_ref = next(it) if has_bias else None
    g_ref, wg_ref, wu_ref, wd_ref = next(it), next(it), next(it), next(it)
    gf_ref = next(it) if final_norm else None
    o_ref = next(it)

    x = x_ref[...]
    if n_pre:
        upd = _dot(pre[0][0][...], pre[0][1][...])
        for a_ref, w_ref in pre[1:]:
            upd = upd + _dot(a_ref[...], w_ref[...])
        if has_bias:
            upd = upd + bias_ref[...]
        x = x + upd
    xn = _rms(x, g_ref[...]).astype(BF16)
    gate = _dot(xn, wg_ref[...])
    up = _dot(xn, wu_ref[...])
    mid = (gate * _sigmoid(gate) * up).astype(BF16)
    y = x + FFN_RES_WEIGHT * _dot(mid, wd_ref[...])
    if final_norm:
        y = _rms(y, gf_ref[...])
    o_ref[...] = y


def _const_spec(shape):
    nd = len(shape)
    return pl.BlockSpec(shape, lambda *_: (0,) * nd, pipeline_mode=pl.Buffered(1))


def _ffn(x2d, g, wg, wu, wd, pre=(), bias=None, final_g=None, tm=512):
    n = x2d.shape[0]
    tm = min(tm, n)
    row = lambda w: pl.BlockSpec((tm, w), lambda i: (i, 0))
    args, specs = [x2d], [row(D_MODEL)]
    for act, w in pre:
        args += [act, w]
        specs += [row(act.shape[1]), _const_spec(w.shape)]
    if bias is not None:
        args.append(bias)
        specs.append(_const_spec(bias.shape))
    args += [g, wg, wu, wd]
    specs += [_const_spec(g.shape), _const_spec(wg.shape), _const_spec(wu.shape), _const_spec(wd.shape)]
    if final_g is not None:
        args.append(final_g)
        specs.append(_const_spec(final_g.shape))
    kern = functools.partial(_ffn_kernel, n_pre=len(pre), has_bias=bias is not None,
                             final_norm=final_g is not None)
    return pl.pallas_call(
        kern, grid=(n // tm,), in_specs=specs, out_specs=row(D_MODEL),
        out_shape=jax.ShapeDtypeStruct((n, D_MODEL), F32),
        compiler_params=pltpu.CompilerParams(dimension_semantics=("arbitrary",),
                                             vmem_limit_bytes=VMEM_LIMIT),
        name="ffn",
    )(*args)


def _rope(z, cos, sin, half, period):
    width = z.shape[-1]
    lane = lax.broadcasted_iota(I32, z.shape, 1)
    first = (lane & (period - 1)) < half
    partner = jnp.where(first, pltpu.roll(z, width - half, 1), pltpu.roll(z, half, 1))
    reps = width // LANES
    if reps > 1:
        cos = jnp.concatenate([cos] * reps, axis=1)
        sin = jnp.concatenate([sin] * reps, axis=1)
    return z * cos + partner * sin


def _mix_in_kernel(x_ref, g_ref, wa_ref, wt_ref, wb_ref, cw_ref, tab_ref,
                   q_ref, qi_ref, k_ref, vt_ref, ki_ref, wit_ref, yc_ref,
                   cbuf_ref, carry_ref):
    s = pl.program_id(0)
    b = pl.program_id(1)
    tm = x_ref.shape[1]
    xn = _rms(x_ref[0], g_ref[...]).astype(BF16)
    za = _dot(xn, wa_ref[...])
    cos_a, sin_a = tab_ref[:, 0:LANES], tab_ref[:, LANES:2 * LANES]
    cos_i, sin_i = tab_ref[:, 2 * LANES:3 * LANES], tab_ref[:, 3 * LANES:4 * LANES]
    half_a = ATT_HEAD_DIM // ROT_FRACTION // 2
    half_i = IDX_DIM // ROT_FRACTION // 2

    q = _rope(za[:, A_Q:A_Q + ATT_WIDTH], cos_a, sin_a, half_a, ATT_HEAD_DIM)
    q_ref[0] = (q * (ATT_HEAD_DIM ** -0.5)).astype(BF16)
    qi = _rope(za[:, A_QI:A_QI + IDX_WIDTH], cos_i, sin_i, half_i, IDX_DIM)
    qi_ref[0] = qi.astype(BF16)
    k = _rope(za[:, A_K:A_K + LANES], cos_a, sin_a, half_a, ATT_HEAD_DIM)
    k_ref[0] = k.astype(BF16)
    ki = _rope(za[:, A_KI:A_KI + LANES], cos_i, sin_i, half_i, IDX_DIM)
    ki_ref[0] = ki[:, :IDX_DIM].astype(BF16)
    zt = _dot_nt(wt_ref[...], xn)
    t_row = lax.broadcasted_iota(I32, zt.shape, 0)
    vt_ref[0, 0] = jnp.where(t_row < ATT_HEAD_DIM, zt,
                             jnp.where(t_row == ATT_HEAD_DIM, 1.0, 0.0)).astype(BF16)
    wit_ref[0] = zt[ATT_HEAD_DIM:, :] * ((IDX_HEADS * IDX_DIM) ** -0.5)

    zb = _dot(xn, wb_ref[...])
    gate_b = zb[:, 0:SC_WIDTH]
    cu = zb[:, SC_WIDTH:2 * SC_WIDTH] * zb[:, 2 * SC_WIDTH:3 * SC_WIDTH]

    @pl.when(s == 0)
    def _():
        cbuf_ref[0:SC_HALO, :] = jnp.zeros((SC_HALO, SC_WIDTH), F32)

    @pl.when(s > 0)
    def _():
        cbuf_ref[0:SC_HALO, :] = carry_ref[b]

    cbuf_ref[SC_HALO:SC_HALO + tm, :] = cu
    carry_ref[b] = cu[tm - SC_HALO:tm, :]
    conv = cu * cw_ref[SC_KERNEL - 1:SC_KERNEL, :]
    for j in range(SC_KERNEL - 1):
        off = SC_HALO - (SC_KERNEL - 1) + j
        conv = conv + cbuf_ref[off:off + tm, :] * cw_ref[j:j + 1, :]
    yc_ref[0] = (gate_b * conv).astype(BF16)


def _mix_in(h, g, wa, wt, wb, conv_w, tab):
    bsz, seq, _ = h.shape
    tm = KEY_CHUNK
    tok = lambda w: pl.BlockSpec((1, tm, w), lambda s, b: (b, s, 0))
    sds = jax.ShapeDtypeStruct
    out_shape = [sds((bsz, seq, ATT_WIDTH), BF16), sds((bsz, seq, IDX_WIDTH), BF16),
                 sds((bsz, seq, LANES), BF16), sds((bsz, seq // tm, VT_ROWS, tm), BF16),
                 sds((bsz, seq, IDX_DIM), BF16), sds((bsz, IDX_HEADS, seq), F32),
                 sds((bsz, seq, SC_WIDTH), BF16)]
    out_specs = [tok(ATT_WIDTH), tok(IDX_WIDTH), tok(LANES),
                 pl.BlockSpec((1, 1, VT_ROWS, tm), lambda s, b: (b, s, 0, 0)),
                 tok(IDX_DIM), pl.BlockSpec((1, IDX_HEADS, tm), lambda s, b: (b, 0, s)),
                 tok(SC_WIDTH)]
    return pl.pallas_call(
        _mix_in_kernel, grid=(seq // tm, bsz),
        in_specs=[tok(D_MODEL), _const_spec(g.shape), _const_spec(wa.shape), _const_spec(wt.shape),
                  _const_spec(wb.shape), _const_spec(conv_w.shape),
                  pl.BlockSpec((tm, 4 * LANES), lambda s, b: (s, 0))],
        out_specs=out_specs, out_shape=out_shape,
        scratch_shapes=[pltpu.VMEM((tm + SC_HALO, SC_WIDTH), F32),
                        pltpu.VMEM((bsz, SC_HALO, SC_WIDTH), F32)],
        compiler_params=pltpu.CompilerParams(dimension_semantics=("arbitrary", "arbitrary"),
                                             vmem_limit_bytes=VMEM_LIMIT),
        name="mix_in",
    )(h, g, wa, wt, wb, conv_w, tab)


def _sortable(x):
    bits = pltpu.bitcast(x + 0.0, I32)
    return bits ^ ((bits >> 31) & 0x7FFFFFFF)


def _count_ge(arr_ref, n_steps, group, cand):
    kc = arr_ref.shape[1]
    c16 = cand.astype(I16)
    one, zero = jnp.int16(1), jnp.int16(0)
    n_acc = 4

    def body(s, accs):
        accs = list(accs)
        for u in range(group):
            for r in range(kc // PACK_ROWS):
                t = arr_ref[s * group + u, r * PACK_ROWS:(r + 1) * PACK_ROWS, :]
                accs[r % n_acc] = accs[r % n_acc] + jnp.where(t >= c16, one, zero)
        return tuple(accs)

    accs = lax.fori_loop(0, n_steps, body,
                         tuple(jnp.zeros((PACK_ROWS, Q_BLOCK), I16) for _ in range(n_acc)))
    total = (accs[0] + accs[1]) + (accs[2] + accs[3])
    return jnp.sum(total.astype(F32), axis=0, keepdims=True)


def _search16(arr_ref, n_steps, group, base, cge0, cgt0, kf):
    def step(t, carry):
        thr, cge, cgt = carry
        cand = thr | (jnp.int32(1) << (15 - t))
        n = base + _count_ge(arr_ref, n_steps, group, cand - HALF_BIAS)
        take = n >= kf
        return jnp.where(take, cand, thr), jnp.where(take, n, cge), jnp.where(take, cgt, n)

    return lax.fori_loop(0, 16, step, (jnp.zeros((1, Q_BLOCK), I32), cge0, cgt0))


def _dsa_kernel(q_ref, qi_ref, wit_ref, k_ref, vt_ref, ki_ref, o_ref,
                qs_ref, qis_ref, hi_ref, lo_ref, low_ref, bias_ref, thr_ref, need_ref, acc_ref,
                lga_ref, lgb_ref, *, topk, group):
    i = pl.program_id(1)
    kc = KEY_CHUNK
    rows = ATT_HEADS * Q_BLOCK
    n_chunks = (i * Q_BLOCK + Q_BLOCK + kc - 1) // kc
    n_pairs = (n_chunks + 1) // 2
    q_pos = i * Q_BLOCK + lax.broadcasted_iota(I32, (kc, Q_BLOCK), 1)
    kf = jnp.float32(topk)

    r_i = lax.broadcasted_iota(I32, (Q_BLOCK, Q_BLOCK), 0)
    c_i = lax.broadcasted_iota(I32, (Q_BLOCK, Q_BLOCK), 1)
    eye = jnp.where(r_i == c_i, 1.0, 0.0).astype(BF16)
    for h in range(ATT_HEADS):
        blk_rows = slice(h * Q_BLOCK, (h + 1) * Q_BLOCK)
        qs_ref[blk_rows, 0:LANES] = jnp.zeros((Q_BLOCK, LANES), BF16)
        qs_ref[blk_rows, 0:ATT_HEAD_DIM] = q_ref[0, :, h * ATT_HEAD_DIM:(h + 1) * ATT_HEAD_DIM]
        qs_ref[blk_rows, LANES:QS_COLS] = eye
    for h in range(IDX_HEADS):
        qis_ref[h * Q_BLOCK:(h + 1) * Q_BLOCK, :] = qi_ref[0, :, h * IDX_DIM:(h + 1) * IDX_DIM]

    wit = wit_ref[0]

    def score_chunk(c):
        start = pl.multiple_of(c * kc, kc)
        ki_c = ki_ref[0, pl.ds(start, kc), :]
        score = jnp.zeros((kc, Q_BLOCK), F32)
        for g in range(IDX_HEADS // GROUP_HEADS):
            dots = _dot_nt(ki_c, qis_ref[g * GROUP_ROWS:(g + 1) * GROUP_ROWS, :])
            for j in range(GROUP_HEADS):
                h = g * GROUP_HEADS + j
                score = score + wit[h:h + 1, :] * jnp.maximum(dots[:, j * Q_BLOCK:(j + 1) * Q_BLOCK], 0.0)
        k_pos = start + lax.broadcasted_iota(I32, (kc, Q_BLOCK), 0)
        key = _sortable(jnp.where(k_pos <= q_pos, score, -jnp.inf))
        hi_ref[c] = (key >> 16).astype(I16)
        lo_ref[c] = ((key & 0xFFFF) - HALF_BIAS).astype(I16)

    def score_pair(p, carry):
        score_chunk(2 * p)
        score_chunk(2 * p + 1)
        return carry

    lax.fori_loop(0, n_pairs, score_pair, 0)

    n_steps = (n_chunks + group - 1) // group

    def pad_chunk(c, carry):
        hi_ref[c] = jnp.full((kc, Q_BLOCK), NEG_INF_KEY >> 16, I16)
        lo_ref[c] = jnp.full((kc, Q_BLOCK), (NEG_INF_KEY & 0xFFFF) - HALF_BIAS, I16)
        return carry

    lax.fori_loop(2 * n_pairs, n_steps * group, pad_chunk, 0)

    @pl.when((i + 1) * Q_BLOCK <= topk)
    def _():
        above = NEG_INF_KEY + 1
        thr_ref[0] = jnp.full((1, Q_BLOCK), above >> 16, I32)
        thr_ref[1] = jnp.full((1, Q_BLOCK), (above & 0xFFFF) - HALF_BIAS, I32)
        need_ref[0] = jnp.zeros((1, Q_BLOCK), F32)
        need_ref[1] = jnp.zeros((1, Q_BLOCK), F32)

    @pl.when((i + 1) * Q_BLOCK > topk)
    def _():
        total = jnp.full((1, Q_BLOCK), n_steps * group * kc, I32).astype(F32)
        zero = jnp.zeros((1, Q_BLOCK), F32)
        t_hi, cge_hi, cgt_hi = _search16(hi_ref, n_steps, group, zero, total, zero, kf)
        t_hi = t_hi - HALF_BIAS
        t_hi16 = t_hi.astype(I16)

        def bucket_step(s, carry):
            for u in range(group):
                c = s * group + u
                low_ref[c] = jnp.where(hi_ref[c] == t_hi16, lo_ref[c], jnp.int16(-HALF_BIAS))
            return carry

        lax.fori_loop(0, n_steps, bucket_step, 0)
        t_lo, cge, cgt = _search16(low_ref, n_steps, group, cgt_hi, cge_hi, cgt_hi, kf)
        thr_ref[0] = t_hi
        thr_ref[1] = t_lo - HALF_BIAS
        need_ref[0] = kf - cgt
        need_ref[1] = cge - cgt

    t_hi16 = thr_ref[0].astype(I16)
    t_lo16 = thr_ref[1].astype(I16)

    def bias_pair(p, carry):
        for c in (2 * p, 2 * p + 1):
            hi, lo = hi_ref[c], lo_ref[c]
            sel = (hi > t_hi16) | ((hi == t_hi16) & (lo >= t_lo16))
            bias_ref[c] = jnp.where(sel, jnp.bfloat16(0.0), jnp.bfloat16(NEG_BIG))
        return carry

    lax.fori_loop(0, n_pairs, bias_pair, 0)

    @pl.when(jnp.max(need_ref[1] - need_ref[0]) > 0.0)
    def _():
        need = need_ref[0]
        r_i = lax.broadcasted_iota(I32, (kc, kc), 0)
        c_i = lax.broadcasted_iota(I32, (kc, kc), 1)
        tri = jnp.where(c_i <= r_i, 1.0, 0.0).astype(BF16)

        def tie_chunk(c, seen):
            eq = (hi_ref[c] == t_hi16) & (lo_ref[c] == t_lo16)
            eq_b = jnp.where(eq, jnp.bfloat16(1.0), jnp.bfloat16(0.0))
            rank = seen + _dot(tri, eq_b)
            drop = (eq_b.astype(F32) > 0.5) & (rank > need)
            bias_ref[c] = jnp.where(drop, NEG_BIG, bias_ref[c].astype(F32)).astype(BF16)
            return rank[kc - 1:kc, :]

        lax.fori_loop(0, n_chunks, tie_chunk, jnp.zeros((1, Q_BLOCK), F32))

    acc_ref[...] = jnp.zeros(acc_ref.shape, F32)

    n_groups = ATT_HEADS // GROUP_HEADS
    group = lambda g: slice(g * GROUP_ROWS, (g + 1) * GROUP_ROWS)

    def keys_of(c):
        start = pl.multiple_of(c * kc, kc)
        return jnp.concatenate([k_ref[0, pl.ds(start, kc), :], bias_ref[c]], axis=1)

    def softmax_step(lg_ref, g, v_c, m):
        lg = lg_ref[:, group(g)]
        m_new = jnp.maximum(m, jnp.max(lg, axis=0, keepdims=True))
        p = jnp.exp(lg - m_new)
        acc_ref[:, group(g)] = jnp.exp(m - m_new) * acc_ref[:, group(g)] + _dot(v_c, p.astype(BF16))
        return m_new

    last = 2 * n_pairs - 1
    lga_ref[...] = _dot_nt(keys_of(0), qs_ref[...])

    def half_step(src_ref, dst_ref, c_src, c_dst, ms):
        keys, v_c = keys_of(c_dst), vt_ref[0, c_src]
        out = []
        for g in range(n_groups):
            dst_ref[:, group(g)] = _dot_nt(keys, qs_ref[group(g), :])
            out.append(softmax_step(src_ref, g, v_c, ms[g]))
        return tuple(out)

    def attn_pair(p, ms):
        ms = half_step(lga_ref, lgb_ref, 2 * p, 2 * p + 1, ms)
        return half_step(lgb_ref, lga_ref, 2 * p + 1, jnp.minimum(2 * p + 2, last), ms)

    lax.fori_loop(0, n_pairs, attn_pair,
                  tuple(jnp.full((1, GROUP_ROWS), NEG_BIG, F32) for _ in range(n_groups)))

    denom = acc_ref[ATT_HEAD_DIM:ATT_HEAD_DIM + 1, :]
    out = (acc_ref[0:ATT_HEAD_DIM, :] / denom).astype(BF16)
    for h in range(ATT_HEADS):
        o_ref[0, :, h * ATT_HEAD_DIM:(h + 1) * ATT_HEAD_DIM] = _dot_nt(
            eye, out[:, h * Q_BLOCK:(h + 1) * Q_BLOCK]).astype(BF16)


def _dsa(q, qi, wi_t, k, v_t, ki):
    bsz, seq, _ = q.shape
    topk = min(TOPK_MAX, seq // 4)
    assert topk % Q_BLOCK == 0 and seq % (2 * KEY_CHUNK) == 0
    n_blk = seq // Q_BLOCK
    n_ck = seq // KEY_CHUNK
    group = SEARCH_GROUP if n_ck % SEARCH_GROUP == 0 else 2
    blk = lambda w: pl.BlockSpec((1, Q_BLOCK, w), lambda b, i: (b, i, 0))
    full = lambda w: pl.BlockSpec((1, seq, w), lambda b, i: (b, 0, 0))
    rows = ATT_HEADS * Q_BLOCK
    return pl.pallas_call(
        functools.partial(_dsa_kernel, topk=topk, group=group), grid=(bsz, n_blk),
        in_specs=[blk(ATT_WIDTH), blk(IDX_WIDTH),
                  pl.BlockSpec((1, IDX_HEADS, Q_BLOCK), lambda b, i: (b, 0, i)),
                  full(LANES),
                  pl.BlockSpec((1, n_ck, VT_ROWS, KEY_CHUNK), lambda b, i: (b, 0, 0, 0)),
                  full(IDX_DIM)],
        out_specs=blk(ATT_WIDTH),
        out_shape=jax.ShapeDtypeStruct((bsz, seq, ATT_WIDTH), BF16),
        scratch_shapes=[pltpu.VMEM((rows, QS_COLS), BF16),
                        pltpu.VMEM((IDX_HEADS * Q_BLOCK, IDX_DIM), BF16),
                        pltpu.VMEM((n_ck, KEY_CHUNK, Q_BLOCK), I16),
                        pltpu.VMEM((n_ck, KEY_CHUNK, Q_BLOCK), I16),
                        pltpu.VMEM((n_ck, KEY_CHUNK, Q_BLOCK), I16),
                        pltpu.VMEM((n_ck, KEY_CHUNK, Q_BLOCK), BF16),
                        pltpu.VMEM((2, 1, Q_BLOCK), I32),
                        pltpu.VMEM((2, 1, Q_BLOCK), F32),
                        pltpu.VMEM((VT_ROWS, rows), F32),
                        pltpu.VMEM((KEY_CHUNK, rows), F32),
                        pltpu.VMEM((KEY_CHUNK, rows), F32)],
        compiler_params=pltpu.CompilerParams(dimension_semantics=("arbitrary", "arbitrary"),
                                             vmem_limit_bytes=VMEM_LIMIT),
        name="dsa",
    )(q, qi, wi_t, k, v_t, ki)


def _conf_in_kernel(x_ref, g_ref, w1_ref, b1_ref, cw_ref, cb_ref, lg_ref, lb_ref, o_ref, ubuf_ref, sh_ref):
    s = pl.program_id(1)
    tm = x_ref.shape[1]
    xn = _rms(x_ref[0], g_ref[...]).astype(BF16)
    z = _dot(xn, w1_ref[...]) + b1_ref[...]
    u = z[:, :CONF_WIDTH] * _sigmoid(z[:, CONF_WIDTH:])

    @pl.when(s == 0)
    def _():
        ubuf_ref[0:CONF_HALO, :] = jnp.zeros((CONF_HALO, CONF_WIDTH), F32)

    @pl.when(s > 0)
    def _():
        ubuf_ref[0:CONF_HALO, :] = ubuf_ref[tm:tm + CONF_HALO, :]

    ubuf_ref[CONF_HALO:CONF_HALO + tm, :] = u
    span = tm + CONF_HALO - SUBLANES
    for r in range(1, SUBLANES):
        sh_ref[r - 1, 0:span, :] = ubuf_ref[r:r + span, :]
    conv = u * cw_ref[CONF_KERNEL - 1:CONF_KERNEL, :] + cb_ref[...]
    for j in range(CONF_KERNEL - 1):
        base, r = divmod(CONF_HALO - (CONF_KERNEL - 1) + j, SUBLANES)
        base *= SUBLANES
        win = ubuf_ref[base:base + tm, :] if r == 0 else sh_ref[r - 1, base:base + tm, :]
        conv = conv + win * cw_ref[j:j + 1, :]
    mu = jnp.mean(conv, axis=-1, keepdims=True)
    xc = conv - mu
    var = jnp.mean(xc * xc, axis=-1, keepdims=True)
    y = xc * lax.rsqrt(var + NORM_EPS) * lg_ref[...] + lb_ref[...]
    o_ref[0] = (y * _sigmoid(y)).astype(BF16)


def _conf_in(h, g, w1, b1, conv_w, conv_b, ln_g, ln_b, tm=256):
    bsz, seq, _ = h.shape
    tm = min(tm, seq)
    tok = lambda w: pl.BlockSpec((1, tm, w), lambda b, s: (b, s, 0))
    consts = (g, w1, b1, conv_w, conv_b, ln_g, ln_b)
    return pl.pallas_call(
        _conf_in_kernel, grid=(bsz, seq // tm),
        in_specs=[tok(D_MODEL)] + [_const_spec(c.shape) for c in consts],
        out_specs=tok(CONF_WIDTH),
        out_shape=jax.ShapeDtypeStruct((bsz, seq, CONF_WIDTH), BF16),
        scratch_shapes=[pltpu.VMEM((tm + CONF_HALO, CONF_WIDTH), F32),
                        pltpu.VMEM((SUBLANES - 1, tm + CONF_HALO - SUBLANES, CONF_WIDTH), F32)],
        compiler_params=pltpu.CompilerParams(dimension_semantics=("arbitrary", "arbitrary"),
                                             vmem_limit_bytes=VMEM_LIMIT),
        name="conf_in",
    )(h, *consts)


def _rope_table(seq, head_dim):
    half = head_dim // ROT_FRACTION // 2
    inv_freq = ROPE_THETA ** (-jnp.arange(half, dtype=F32) / half)
    ang = jnp.arange(seq, dtype=F32)[:, None] * inv_freq[None, :]
    cos, sin = jnp.cos(ang), jnp.sin(ang)
    rest = head_dim - 2 * half
    cos_h = jnp.concatenate([cos, cos, jnp.ones((seq, rest), F32)], axis=1)
    sin_h = jnp.concatenate([-sin, sin, jnp.zeros((seq, rest), F32)], axis=1)
    reps = LANES // head_dim
    return jnp.tile(cos_h, (1, reps)), jnp.tile(sin_h, (1, reps))


def _pack_w_in(w_in):
    sizes = (ATT_WIDTH, ATT_HEAD_DIM, ATT_HEAD_DIM, IDX_WIDTH, IDX_DIM, IDX_HEADS)
    offs = [0]
    for sz in sizes:
        offs.append(offs[-1] + sz)
    q, k, v, qi, ki, wi = (w_in[:, offs[j]:offs[j + 1]] for j in range(len(sizes)))
    pad = lambda w: jnp.pad(w, ((0, 0), (0, LANES - w.shape[1])))
    wa = jnp.concatenate([q, qi, pad(k), pad(ki)], axis=1)
    wt = jnp.concatenate([v, wi], axis=1).T
    return wa.astype(BF16), wt.astype(BF16), w_in[:, offs[-1]:].astype(BF16)


def kernel(x, ffn_norm, ffn_w_gate, ffn_w_up, ffn_w_down, mix_norm, hyb_w_in, hyb_conv_w, hyb_w_out,
           conf_w_pw1, conf_b_pw1, conf_conv_w, conf_conv_b, conf_ln_g, conf_ln_b, conf_w_pw2,
           conf_b_pw2, final_norm):
    bsz, seq, _ = x.shape
    n = bsz * seq
    depth = ffn_norm.shape[0]
    row = lambda a: a.reshape(1, -1)
    ffn_w = lambda layer, j: (row(ffn_norm[layer, j]), ffn_w_gate[layer, j].astype(BF16),
                              ffn_w_up[layer, j].astype(BF16), ffn_w_down[layer, j].astype(BF16))
    cos_a, sin_a = _rope_table(seq, ATT_HEAD_DIM)
    cos_i, sin_i = _rope_table(seq, IDX_DIM)
    tab = jnp.concatenate([cos_a, sin_a, cos_i, sin_i], axis=1)

    h = x.reshape(n, D_MODEL)
    for layer in range(depth):
        last = layer == depth - 1
        h = _ffn(h, *ffn_w(layer, 0))
        h3 = h.reshape(bsz, seq, D_MODEL)
        if layer % 2 == 0:
            e = layer // 2
            wa, wt, wb = _pack_w_in(hyb_w_in[e])
            q, qi, k, v_t, ki, wi_t, y_conv = _mix_in(h3, row(mix_norm[layer]), wa, wt, wb,
                                                      hyb_conv_w[e], tab)
            y_attn = _dsa(q, qi, wi_t, k, v_t, ki)
            w_out = hyb_w_out[e].astype(BF16)
            pre = ((y_attn.reshape(n, ATT_WIDTH), w_out[:ATT_WIDTH]),
                   (y_conv.reshape(n, SC_WIDTH), w_out[ATT_WIDTH:]))
            bias = None
        else:
            o = layer // 2
            u = _conf_in(h3, row(mix_norm[layer]), conf_w_pw1[o].astype(BF16), row(conf_b_pw1[o]),
                         conf_conv_w[o], row(conf_conv_b[o]), row(conf_ln_g[o]), row(conf_ln_b[o]))
            pre = ((u.reshape(n, CONF_WIDTH), conf_w_pw2[o].astype(BF16)),)
            bias = row(conf_b_pw2[o])
        h = _ffn(h, *ffn_w(layer, 1), pre=pre, bias=bias,
                 final_g=row(final_norm) if last else None)
    return h.reshape(bsz, seq, D_MODEL)
```

```python
import functools

import jax
import jax.numpy as jnp
from jax import lax
from jax.experimental import pallas as pl
from jax.experimental.pallas import tpu as pltpu

F32 = jnp.float32
BF16 = jnp.bfloat16
I32 = jnp.int32
I16 = jnp.int16

D_MODEL = 1024
D_FF = 2816
FFN_RES_WEIGHT = 0.5
ATT_HEADS = 8
ATT_HEAD_DIM = 64
ATT_WIDTH = ATT_HEADS * ATT_HEAD_DIM
IDX_HEADS = 8
IDX_DIM = 32
IDX_WIDTH = IDX_HEADS * IDX_DIM
TOPK_MAX = 256
Q_BLOCK = 128
SC_WIDTH = D_MODEL - ATT_WIDTH
SC_KERNEL = 3
CONF_WIDTH = D_MODEL
CONF_KERNEL = 31
ROPE_THETA = 500000.0
ROT_FRACTION = 4
NORM_EPS = 1e-6

LANES = 128
SUBLANES = 8
PACK_ROWS = 16
VMEM_LIMIT = 56 * 1024 * 1024

A_Q = 0
A_QI = A_Q + ATT_WIDTH
A_K = A_QI + IDX_WIDTH
A_KI = A_K + LANES
A_COLS = A_KI + LANES

KEY_CHUNK = 512
VT_ROWS = ATT_HEAD_DIM + IDX_HEADS
QS_COLS = 2 * LANES
SEARCH_GROUP = 1
LOG2E = 1.4426950408889634
GROUP_HEADS = 2
GROUP_ROWS = GROUP_HEADS * Q_BLOCK
NEG_BIG = -1e30
HALF_BIAS = 1 << 15
NEG_INF_KEY = -2 ** 31 + 0x007FFFFF
CONF_HALO = 32
SC_HALO = 8


def _dot(a, b):
    return jnp.dot(a, b, preferred_element_type=F32)


def _dot_nt(a, b):
    return lax.dot_general(a, b, (((1,), (1,)), ((), ())), preferred_element_type=F32)


def _rms(x, g):
    return x * lax.rsqrt(jnp.mean(x * x, axis=-1, keepdims=True) + NORM_EPS) * g


def _sigmoid(x):
    return 1.0 / (1.0 + jnp.exp(-x))


def _ffn_kernel(*refs, n_pre, has_bias, final_norm):
    it = iter(refs)
    x_ref = next(it)
    pre = [(next(it), next(it)) for _ in range(n_pre)]
    bias_ref = next(it) if has_bias else None
    g_ref, wg_ref, wu_ref, wd_ref = next(it), next(it), next(it), next(it)
    gf_ref = next(it) if final_norm else None
    o_ref = next(it)

    x = x_ref[...]
    if n_pre:
        upd = _dot(pre[0][0][...], pre[0][1][...])
        for a_ref, w_ref in pre[1:]:
            upd = upd + _dot(a_ref[...], w_ref[...])
        if has_bias:
            upd = upd + bias_ref[...]
        x = x + upd
    xn = _rms(x, g_ref[...]).astype(BF16)
    gate = _dot(xn, wg_ref[...])
    up = _dot(xn, wu_ref[...])
    mid = (gate * _sigmoid(gate) * up).astype(BF16)
    y = x + FFN_RES_WEIGHT * _dot(mid, wd_ref[...])
    if final_norm:
        y = _rms(y, gf_ref[...])
    o_ref[...] = y


def _const_spec(shape):
    nd = len(shape)
    return pl.BlockSpec(shape, lambda *_: (0,) * nd, pipeline_mode=pl.Buffered(1))


def _ffn(x2d, g, wg, wu, wd, pre=(), bias=None, final_g=None, tm=512):
    n = x2d.shape[0]
    tm = min(tm, n)
    row = lambda w: pl.BlockSpec((tm, w), lambda i: (i, 0))
    args, specs = [x2d], [row(D_MODEL)]
    for act, w in pre:
        args += [act, w]
        specs += [row(act.shape[1]), _const_spec(w.shape)]
    if bias is not None:
        args.append(bias)
        specs.append(_const_spec(bias.shape))
    args += [g, wg, wu, wd]
    specs += [_const_spec(g.shape), _const_spec(wg.shape), _const_spec(wu.shape), _const_spec(wd.shape)]
    if final_g is not None:
        args.append(final_g)
        specs.append(_const_spec(final_g.shape))
    kern = functools.partial(_ffn_kernel, n_pre=len(pre), has_bias=bias is not None,
                             final_norm=final_g is not None)
    return pl.pallas_call(
        kern, grid=(n // tm,), in_specs=specs, out_specs=row(D_MODEL),
        out_shape=jax.ShapeDtypeStruct((n, D_MODEL), F32),
        compiler_params=pltpu.CompilerParams(dimension_semantics=("arbitrary",),
                                             vmem_limit_bytes=VMEM_LIMIT),
        name="ffn",
    )(*args)


def _rope(z, cos, sin, half, period):
    width = z.shape[-1]
    lane = lax.broadcasted_iota(I32, z.shape, 1)
    first = (lane & (period - 1)) < half
    partner = jnp.where(first, pltpu.roll(z, width - half, 1), pltpu.roll(z, half, 1))
    reps = width // LANES
    if reps > 1:
        cos = jnp.concatenate([cos] * reps, axis=1)
        sin = jnp.concatenate([sin] * reps, axis=1)
    return z * cos + partner * sin


def _mix_in_kernel(x_ref, g_ref, wa_ref, wt_ref, wb_ref, cw_ref, tab_ref,
                   q_ref, qi_ref, k_ref, vt_ref, ki_ref, wit_ref, yc_ref,
                   cbuf_ref, carry_ref):
    s = pl.program_id(0)
    b = pl.program_id(1)
    tm = x_ref.shape[1]
    xn = _rms(x_ref[0], g_ref[...]).astype(BF16)
    za = _dot(xn, wa_ref[...])
    cos_a, sin_a = tab_ref[:, 0:LANES], tab_ref[:, LANES:2 * LANES]
    cos_i, sin_i = tab_ref[:, 2 * LANES:3 * LANES], tab_ref[:, 3 * LANES:4 * LANES]
    half_a = ATT_HEAD_DIM // ROT_FRACTION // 2
    half_i = IDX_DIM // ROT_FRACTION // 2

    q = _rope(za[:, A_Q:A_Q + ATT_WIDTH], cos_a, sin_a, half_a, ATT_HEAD_DIM)
    q_ref[0] = (q * (ATT_HEAD_DIM ** -0.5 * LOG2E)).astype(BF16)
    qi = _rope(za[:, A_QI:A_QI + IDX_WIDTH], cos_i, sin_i, half_i, IDX_DIM)
    qi_ref[0] = qi.astype(BF16)
    k = _rope(za[:, A_K:A_K + LANES], cos_a, sin_a, half_a, ATT_HEAD_DIM)
    k_ref[0] = k.astype(BF16)
    ki = _rope(za[:, A_KI:A_KI + LANES], cos_i, sin_i, half_i, IDX_DIM)
    ki_ref[0] = ki[:, :IDX_DIM].astype(BF16)
    zt = _dot_nt(wt_ref[...], xn)
    t_row = lax.broadcasted_iota(I32, zt.shape, 0)
    vt_ref[0, 0] = jnp.where(t_row < ATT_HEAD_DIM, zt,
                             jnp.where(t_row == ATT_HEAD_DIM, 1.0, 0.0)).astype(BF16)
    wit_ref[0] = zt[ATT_HEAD_DIM:, :] * ((IDX_HEADS * IDX_DIM) ** -0.5)

    zb = _dot(xn, wb_ref[...])
    gate_b = zb[:, 0:SC_WIDTH]
    cu = zb[:, SC_WIDTH:2 * SC_WIDTH] * zb[:, 2 * SC_WIDTH:3 * SC_WIDTH]

    @pl.when(s == 0)
    def _():
        cbuf_ref[0:SC_HALO, :] = jnp.zeros((SC_HALO, SC_WIDTH), F32)

    @pl.when(s > 0)
    def _():
        cbuf_ref[0:SC_HALO, :] = carry_ref[b]

    cbuf_ref[SC_HALO:SC_HALO + tm, :] = cu
    carry_ref[b] = cu[tm - SC_HALO:tm, :]
    conv = cu * cw_ref[SC_KERNEL - 1:SC_KERNEL, :]
    for j in range(SC_KERNEL - 1):
        off = SC_HALO - (SC_KERNEL - 1) + j
        conv = conv + cbuf_ref[off:off + tm, :] * cw_ref[j:j + 1, :]
    yc_ref[0] = (gate_b * conv).astype(BF16)


def _mix_in(h, g, wa, wt, wb, conv_w, tab):
    bsz, seq, _ = h.shape
    tm = KEY_CHUNK
    tok = lambda w: pl.BlockSpec((1, tm, w), lambda s, b: (b, s, 0))
    sds = jax.ShapeDtypeStruct
    out_shape = [sds((bsz, seq, ATT_WIDTH), BF16), sds((bsz, seq, IDX_WIDTH), BF16),
                 sds((bsz, seq, LANES), BF16), sds((bsz, seq // tm, VT_ROWS, tm), BF16),
                 sds((bsz, seq, IDX_DIM), BF16), sds((bsz, IDX_HEADS, seq), F32),
                 sds((bsz, seq, SC_WIDTH), BF16)]
    out_specs = [tok(ATT_WIDTH), tok(IDX_WIDTH), tok(LANES),
                 pl.BlockSpec((1, 1, VT_ROWS, tm), lambda s, b: (b, s, 0, 0)),
                 tok(IDX_DIM), pl.BlockSpec((1, IDX_HEADS, tm), lambda s, b: (b, 0, s)),
                 tok(SC_WIDTH)]
    return pl.pallas_call(
        _mix_in_kernel, grid=(seq // tm, bsz),
        in_specs=[tok(D_MODEL), _const_spec(g.shape), _const_spec(wa.shape), _const_spec(wt.shape),
                  _const_spec(wb.shape), _const_spec(conv_w.shape),
                  pl.BlockSpec((tm, 4 * LANES), lambda s, b: (s, 0))],
        out_specs=out_specs, out_shape=out_shape,
        scratch_shapes=[pltpu.VMEM((tm + SC_HALO, SC_WIDTH), F32),
                        pltpu.VMEM((bsz, SC_HALO, SC_WIDTH), F32)],
        compiler_params=pltpu.CompilerParams(dimension_semantics=("arbitrary", "arbitrary"),
                                             vmem_limit_bytes=VMEM_LIMIT),
        name="mix_in",
    )(h, g, wa, wt, wb, conv_w, tab)


def _sortable(x):
    bits = pltpu.bitcast(x + 0.0, I32)
    return bits ^ ((bits >> 31) & 0x7FFFFFFF)


def _count_ge(arr_ref, n_steps, group, cand):
    kc = arr_ref.shape[1]
    c16 = cand.astype(I16)
    one, zero = jnp.int16(1), jnp.int16(0)
    n_acc = 4

    def body(s, accs):
        accs = list(accs)
        for u in range(group):
            for r in range(kc // PACK_ROWS):
                t = arr_ref[s * group + u, r * PACK_ROWS:(r + 1) * PACK_ROWS, :]
                accs[r % n_acc] = accs[r % n_acc] + jnp.where(t >= c16, one, zero)
        return tuple(accs)

    accs = lax.fori_loop(0, n_steps, body,
                         tuple(jnp.zeros((PACK_ROWS, Q_BLOCK), I16) for _ in range(n_acc)))
    total = (accs[0] + accs[1]) + (accs[2] + accs[3])
    return jnp.sum(total.astype(F32), axis=0, keepdims=True)


def _search16(arr_ref, n_steps, group, base, cge0, cgt0, kf):
    def step(t, carry):
        thr, cge, cgt = carry
        cand = thr | (jnp.int32(1) << (15 - t))
        n = base + _count_ge(arr_ref, n_steps, group, cand - HALF_BIAS)
        take = n >= kf
        return jnp.where(take, cand, thr), jnp.where(take, n, cge), jnp.where(take, cgt, n)

    return lax.fori_loop(0, 16, step, (jnp.zeros((1, Q_BLOCK), I32), cge0, cgt0))


def _dsa_kernel(q_ref, qi_ref, wit_ref, k_ref, vt_ref, ki_ref, o_ref,
                qs_ref, qis_ref, hi_ref, lo_ref, low_ref, bias_ref, thr_ref, need_ref, acc_ref,
                lga_ref, lgb_ref, *, topk, group):
    i = pl.program_id(1)
    kc = KEY_CHUNK
    rows = ATT_HEADS * Q_BLOCK
    n_chunks = (i * Q_BLOCK + Q_BLOCK + kc - 1) // kc
    n_pairs = (n_chunks + 1) // 2
    q_pos = i * Q_BLOCK + lax.broadcasted_iota(I32, (kc, Q_BLOCK), 1)
    kf = jnp.float32(topk)

    r_i = lax.broadcasted_iota(I32, (Q_BLOCK, Q_BLOCK), 0)
    c_i = lax.broadcasted_iota(I32, (Q_BLOCK, Q_BLOCK), 1)
    eye = jnp.where(r_i == c_i, 1.0, 0.0).astype(BF16)
    for h in range(ATT_HEADS):
        blk_rows = slice(h * Q_BLOCK, (h + 1) * Q_BLOCK)
        qs_ref[blk_rows, 0:LANES] = jnp.zeros((Q_BLOCK, LANES), BF16)
        qs_ref[blk_rows, 0:ATT_HEAD_DIM] = q_ref[0, :, h * ATT_HEAD_DIM:(h + 1) * ATT_HEAD_DIM]
        qs_ref[blk_rows, LANES:QS_COLS] = eye
    for h in range(IDX_HEADS):
        qis_ref[h * Q_BLOCK:(h + 1) * Q_BLOCK, :] = qi_ref[0, :, h * IDX_DIM:(h + 1) * IDX_DIM]

    wit = wit_ref[0]

    def score_chunk(c):
        start = pl.multiple_of(c * kc, kc)
        ki_c = ki_ref[0, pl.ds(start, kc), :]
        score = jnp.zeros((kc, Q_BLOCK), F32)
        for g in range(IDX_HEADS // GROUP_HEADS):
            dots = _dot_nt(ki_c, qis_ref[g * GROUP_ROWS:(g + 1) * GROUP_ROWS, :])
            for j in range(GROUP_HEADS):
                h = g * GROUP_HEADS + j
                score = score + wit[h:h + 1, :] * jnp.maximum(dots[:, j * Q_BLOCK:(j + 1) * Q_BLOCK], 0.0)
        k_pos = start + lax.broadcasted_iota(I32, (kc, Q_BLOCK), 0)
        key = _sortable(jnp.where(k_pos <= q_pos, score, -jnp.inf))
        hi_ref[c] = (key >> 16).astype(I16)
        lo_ref[c] = ((key & 0xFFFF) - HALF_BIAS).astype(I16)

    def score_pair(p, carry):
        score_chunk(2 * p)
        score_chunk(2 * p + 1)
        return carry

    lax.fori_loop(0, n_pairs, score_pair, 0)

    n_steps = (n_chunks + group - 1) // group

    def pad_chunk(c, carry):
        hi_ref[c] = jnp.full((kc, Q_BLOCK), NEG_INF_KEY >> 16, I16)
        lo_ref[c] = jnp.full((kc, Q_BLOCK), (NEG_INF_KEY & 0xFFFF) - HALF_BIAS, I16)
        return carry

    lax.fori_loop(2 * n_pairs, n_steps * group, pad_chunk, 0)

    @pl.when((i + 1) * Q_BLOCK <= topk)
    def _():
        above = NEG_INF_KEY + 1
        thr_ref[0] = jnp.full((1, Q_BLOCK), above >> 16, I32)
        thr_ref[1] = jnp.full((1, Q_BLOCK), (above & 0xFFFF) - HALF_BIAS, I32)
        need_ref[0] = jnp.zeros((1, Q_BLOCK), F32)
        need_ref[1] = jnp.zeros((1, Q_BLOCK), F32)

    @pl.when((i + 1) * Q_BLOCK > topk)
    def _():
        total = jnp.full((1, Q_BLOCK), n_steps * group * kc, I32).astype(F32)
        zero = jnp.zeros((1, Q_BLOCK), F32)
        t_hi, cge_hi, cgt_hi = _search16(hi_ref, n_steps, group, zero, total, zero, kf)
        t_hi = t_hi - HALF_BIAS
        t_hi16 = t_hi.astype(I16)

        def bucket_step(s, carry):
            for u in range(group):
                c = s * group + u
                low_ref[c] = jnp.where(hi_ref[c] == t_hi16, lo_ref[c], jnp.int16(-HALF_BIAS))
            return carry

        lax.fori_loop(0, n_steps, bucket_step, 0)
        t_lo, cge, cgt = _search16(low_ref, n_steps, group, cgt_hi, cge_hi, cgt_hi, kf)
        thr_ref[0] = t_hi
        thr_ref[1] = t_lo - HALF_BIAS
        need_ref[0] = kf - cgt
        need_ref[1] = cge - cgt

    t_hi16 = thr_ref[0].astype(I16)
    t_lo16 = thr_ref[1].astype(I16)

    def bias_pair(p, carry):
        for c in (2 * p, 2 * p + 1):
            hi, lo = hi_ref[c], lo_ref[c]
            sel = (hi > t_hi16) | ((hi == t_hi16) & (lo >= t_lo16))
            bias_ref[c] = jnp.where(sel, jnp.bfloat16(0.0), jnp.bfloat16(NEG_BIG))
        return carry

    lax.fori_loop(0, n_pairs, bias_pair, 0)

    @pl.when(jnp.max(need_ref[1] - need_ref[0]) > 0.0)
    def _():
        need = need_ref[0]
        r_i = lax.broadcasted_iota(I32, (kc, kc), 0)
        c_i = lax.broadcasted_iota(I32, (kc, kc), 1)
        tri = jnp.where(c_i <= r_i, 1.0, 0.0).astype(BF16)

        def tie_chunk(c, seen):
            eq = (hi_ref[c] == t_hi16) & (lo_ref[c] == t_lo16)
            eq_b = jnp.where(eq, jnp.bfloat16(1.0), jnp.bfloat16(0.0))
            rank = seen + _dot(tri, eq_b)
            drop = (eq_b.astype(F32) > 0.5) & (rank > need)
            bias_ref[c] = jnp.where(drop, NEG_BIG, bias_ref[c].astype(F32)).astype(BF16)
            return rank[kc - 1:kc, :]

        lax.fori_loop(0, n_chunks, tie_chunk, jnp.zeros((1, Q_BLOCK), F32))

    acc_ref[...] = jnp.zeros(acc_ref.shape, F32)

    n_groups = ATT_HEADS // GROUP_HEADS
    group = lambda g: slice(g * GROUP_ROWS, (g + 1) * GROUP_ROWS)

    def keys_of(c):
        start = pl.multiple_of(c * kc, kc)
        return jnp.concatenate([k_ref[0, pl.ds(start, kc), :], bias_ref[c]], axis=1)

    def softmax_step(lg_ref, g, v_c, m):
        lg = lg_ref[:, group(g)]
        m_new = jnp.maximum(m, jnp.max(lg, axis=0, keepdims=True))
        p = jnp.exp2(lg - m_new)
        acc_ref[:, group(g)] = jnp.exp2(m - m_new) * acc_ref[:, group(g)] + _dot(v_c, p.astype(BF16))
        return m_new

    last = 2 * n_pairs - 1
    lga_ref[...] = _dot_nt(keys_of(0), qs_ref[...])

    def half_step(src_ref, dst_ref, c_src, c_dst, ms):
        keys, v_c = keys_of(c_dst), vt_ref[0, c_src]
        out = []
        for g in range(n_groups):
            dst_ref[:, group(g)] = _dot_nt(keys, qs_ref[group(g), :])
            out.append(softmax_step(src_ref, g, v_c, ms[g]))
        return tuple(out)

    def attn_pair(p, ms):
        ms = half_step(lga_ref, lgb_ref, 2 * p, 2 * p + 1, ms)
        return half_step(lgb_ref, lga_ref, 2 * p + 1, jnp.minimum(2 * p + 2, last), ms)

    lax.fori_loop(0, n_pairs, attn_pair,
                  tuple(jnp.full((1, GROUP_ROWS), NEG_BIG, F32) for _ in range(n_groups)))

    denom = acc_ref[ATT_HEAD_DIM:ATT_HEAD_DIM + 1, :]
    out = (acc_ref[0:ATT_HEAD_DIM, :] / denom).astype(BF16)
    for h in range(ATT_HEADS):
        o_ref[0, :, h * ATT_HEAD_DIM:(h + 1) * ATT_HEAD_DIM] = _dot_nt(
            eye, out[:, h * Q_BLOCK:(h + 1) * Q_BLOCK]).astype(BF16)


def _dsa(q, qi, wi_t, k, v_t, ki):
    bsz, seq, _ = q.shape
    topk = min(TOPK_MAX, seq // 4)
    assert topk % Q_BLOCK == 0 and seq % (2 * KEY_CHUNK) == 0
    n_blk = seq // Q_BLOCK
    n_ck = seq // KEY_CHUNK
    group = SEARCH_GROUP if n_ck % SEARCH_GROUP == 0 else 2
    blk = lambda w: pl.BlockSpec((1, Q_BLOCK, w), lambda b, i: (b, i, 0))
    full = lambda w: pl.BlockSpec((1, seq, w), lambda b, i: (b, 0, 0))
    rows = ATT_HEADS * Q_BLOCK
    return pl.pallas_call(
        functools.partial(_dsa_kernel, topk=topk, group=group), grid=(bsz, n_blk),
        in_specs=[blk(ATT_WIDTH), blk(IDX_WIDTH),
                  pl.BlockSpec((1, IDX_HEADS, Q_BLOCK), lambda b, i: (b, 0, i)),
                  full(LANES),
                  pl.BlockSpec((1, n_ck, VT_ROWS, KEY_CHUNK), lambda b, i: (b, 0, 0, 0)),
                  full(IDX_DIM)],
        out_specs=blk(ATT_WIDTH),
        out_shape=jax.ShapeDtypeStruct((bsz, seq, ATT_WIDTH), BF16),
        scratch_shapes=[pltpu.VMEM((rows, QS_COLS), BF16),
                        pltpu.VMEM((IDX_HEADS * Q_BLOCK, IDX_DIM), BF16),
                        pltpu.VMEM((n_ck, KEY_CHUNK, Q_BLOCK), I16),
                        pltpu.VMEM((n_ck, KEY_CHUNK, Q_BLOCK), I16),
                        pltpu.VMEM((n_ck, KEY_CHUNK, Q_BLOCK), I16),
                        pltpu.VMEM((n_ck, KEY_CHUNK, Q_BLOCK), BF16),
                        pltpu.VMEM((2, 1, Q_BLOCK), I32),
                        pltpu.VMEM((2, 1, Q_BLOCK), F32),
                        pltpu.VMEM((VT_ROWS, rows), F32),
                        pltpu.VMEM((KEY_CHUNK, rows), F32),
                        pltpu.VMEM((KEY_CHUNK, rows), F32)],
        compiler_params=pltpu.CompilerParams(dimension_semantics=("arbitrary", "arbitrary"),
                                             vmem_limit_bytes=VMEM_LIMIT),
        name="dsa",
    )(q, qi, wi_t, k, v_t, ki)


def _conf_in_kernel(x_ref, g_ref, w1_ref, b1_ref, cw_ref, cb_ref, lg_ref, lb_ref, o_ref, ubuf_ref, sh_ref):
    s = pl.program_id(1)
    tm = x_ref.shape[1]
    xn = _rms(x_ref[0], g_ref[...]).astype(BF16)
    z = _dot(xn, w1_ref[...]) + b1_ref[...]
    u = z[:, :CONF_WIDTH] * _sigmoid(z[:, CONF_WIDTH:])

    @pl.when(s == 0)
    def _():
        ubuf_ref[0:CONF_HALO, :] = jnp.zeros((CONF_HALO, CONF_WIDTH), F32)

    @pl.when(s > 0)
    def _():
        ubuf_ref[0:CONF_HALO, :] = ubuf_ref[tm:tm + CONF_HALO, :]

    ubuf_ref[CONF_HALO:CONF_HALO + tm, :] = u
    span = tm + CONF_HALO - SUBLANES
    for r in range(1, SUBLANES):
        sh_ref[r - 1, 0:span, :] = ubuf_ref[r:r + span, :]
    conv = u * cw_ref[CONF_KERNEL - 1:CONF_KERNEL, :] + cb_ref[...]
    for j in range(CONF_KERNEL - 1):
        base, r = divmod(CONF_HALO - (CONF_KERNEL - 1) + j, SUBLANES)
        base *= SUBLANES
        win = ubuf_ref[base:base + tm, :] if r == 0 else sh_ref[r - 1, base:base + tm, :]
        conv = conv + win * cw_ref[j:j + 1, :]
    mu = jnp.mean(conv, axis=-1, keepdims=True)
    xc = conv - mu
    var = jnp.mean(xc * xc, axis=-1, keepdims=True)
    y = xc * lax.rsqrt(var + NORM_EPS) * lg_ref[...] + lb_ref[...]
    o_ref[0] = (y * _sigmoid(y)).astype(BF16)


def _conf_in(h, g, w1, b1, conv_w, conv_b, ln_g, ln_b, tm=256):
    bsz, seq, _ = h.shape
    tm = min(tm, seq)
    tok = lambda w: pl.BlockSpec((1, tm, w), lambda b, s: (b, s, 0))
    consts = (g, w1, b1, conv_w, conv_b, ln_g, ln_b)
    return pl.pallas_call(
        _conf_in_kernel, grid=(bsz, seq // tm),
        in_specs=[tok(D_MODEL)] + [_const_spec(c.shape) for c in consts],
        out_specs=tok(CONF_WIDTH),
        out_shape=jax.ShapeDtypeStruct((bsz, seq, CONF_WIDTH), BF16),
        scratch_shapes=[pltpu.VMEM((tm + CONF_HALO, CONF_WIDTH), F32),
                        pltpu.VMEM((SUBLANES - 1, tm + CONF_HALO - SUBLANES, CONF_WIDTH), F32)],
        compiler_params=pltpu.CompilerParams(dimension_semantics=("arbitrary", "arbitrary"),
                                             vmem_limit_bytes=VMEM_LIMIT),
        name="conf_in",
    )(h, *consts)


def _rope_table(seq, head_dim):
    half = head_dim // ROT_FRACTION // 2
    inv_freq = ROPE_THETA ** (-jnp.arange(half, dtype=F32) / half)
    ang = jnp.arange(seq, dtype=F32)[:, None] * inv_freq[None, :]
    cos, sin = jnp.cos(ang), jnp.sin(ang)
    rest = head_dim - 2 * half
    cos_h = jnp.concatenate([cos, cos, jnp.ones((seq, rest), F32)], axis=1)
    sin_h = jnp.concatenate([-sin, sin, jnp.zeros((seq, rest), F32)], axis=1)
    reps = LANES // head_dim
    return jnp.tile(cos_h, (1, reps)), jnp.tile(sin_h, (1, reps))


def _pack_w_in(w_in):
    sizes = (ATT_WIDTH, ATT_HEAD_DIM, ATT_HEAD_DIM, IDX_WIDTH, IDX_DIM, IDX_HEADS)
    offs = [0]
    for sz in sizes:
        offs.append(offs[-1] + sz)
    q, k, v, qi, ki, wi = (w_in[:, offs[j]:offs[j + 1]] for j in range(len(sizes)))
    pad = lambda w: jnp.pad(w, ((0, 0), (0, LANES - w.shape[1])))
    wa = jnp.concatenate([q, qi, pad(k), pad(ki)], axis=1)
    wt = jnp.concatenate([v, wi], axis=1).T
    return wa.astype(BF16), wt.astype(BF16), w_in[:, offs[-1]:].astype(BF16)


def kernel(x, ffn_norm, ffn_w_gate, ffn_w_up, ffn_w_down, mix_norm, hyb_w_in, hyb_conv_w, hyb_w_out,
           conf_w_pw1, conf_b_pw1, conf_conv_w, conf_conv_b, conf_ln_g, conf_ln_b, conf_w_pw2,
           conf_b_pw2, final_norm):
    bsz, seq, _ = x.shape
    n = bsz * seq
    depth = ffn_norm.shape[0]
    row = lambda a: a.reshape(1, -1)
    ffn_w = lambda layer, j: (row(ffn_norm[layer, j]), ffn_w_gate[layer, j].astype(BF16),
                              ffn_w_up[layer, j].astype(BF16), ffn_w_down[layer, j].astype(BF16))
    cos_a, sin_a = _rope_table(seq, ATT_HEAD_DIM)
    cos_i, sin_i = _rope_table(seq, IDX_DIM)
    tab = jnp.concatenate([cos_a, sin_a, cos_i, sin_i], axis=1)

    h = x.reshape(n, D_MODEL)
    for layer in range(depth):
        last = layer == depth - 1
        h = _ffn(h, *ffn_w(layer, 0))
        h3 = h.reshape(bsz, seq, D_MODEL)
        if layer % 2 == 0:
            e = layer // 2
            wa, wt, wb = _pack_w_in(hyb_w_in[e])
            q, qi, k, v_t, ki, wi_t, y_conv = _mix_in(h3, row(mix_norm[layer]), wa, wt, wb,
                                                      hyb_conv_w[e], tab)
            y_attn = _dsa(q, qi, wi_t, k, v_t, ki)
            w_out = hyb_w_out[e].astype(BF16)
            pre = ((y_attn.reshape(n, ATT_WIDTH), w_out[:ATT_WIDTH]),
                   (y_conv.reshape(n, SC_WIDTH), w_out[ATT_WIDTH:]))
            bias = None
        else:
            o = layer // 2
            u = _conf_in(h3, row(mix_norm[layer]), conf_w_pw1[o].astype(BF16), row(conf_b_pw1[o]),
                         conf_conv_w[o], row(conf_conv_b[o]), row(conf_ln_g[o]), row(conf_ln_b[o]))
            pre = ((u.reshape(n, CONF_WIDTH), conf_w_pw2[o].astype(BF16)),)
            bias = row(conf_b_pw2[o])
        h = _ffn(h, *ffn_w(layer, 1), pre=pre, bias=bias,
                 final_g=row(final_norm) if last else None)
    return h.reshape(bsz, seq, D_MODEL)
```

```python
import functools

import jax
import jax.numpy as jnp
from jax import lax
from jax.experimental import pallas as pl
from jax.experimental.pallas import tpu as pltpu

F32 = jnp.float32
BF16 = jnp.bfloat16
I32 = jnp.int32
I16 = jnp.int16

D_MODEL = 1024
D_FF = 2816
FFN_RES_WEIGHT = 0.5
ATT_HEADS = 8
ATT_HEAD_DIM = 64
ATT_WIDTH = ATT_HEADS * ATT_HEAD_DIM
IDX_HEADS = 8
IDX_DIM = 32
IDX_WIDTH = IDX_HEADS * IDX_DIM
TOPK_MAX = 256
Q_BLOCK = 128
SC_WIDTH = D_MODEL - ATT_WIDTH
SC_KERNEL = 3
CONF_WIDTH = D_MODEL
CONF_KERNEL = 31
ROPE_THETA = 500000.0
ROT_FRACTION = 4
NORM_EPS = 1e-6

LANES = 128
SUBLANES = 8
PACK_ROWS = 16
VMEM_LIMIT = 56 * 1024 * 1024

A_Q = 0
A_QI = A_Q + ATT_WIDTH
A_K = A_QI + IDX_WIDTH
A_KI = A_K + LANES
A_COLS = A_KI + LANES

KEY_CHUNK = 512
VT_ROWS = ATT_HEAD_DIM + IDX_HEADS
QS_COLS = 2 * LANES
SEARCH_GROUP = 1
LO_FIRST_BITS = 11
LOG2E = 1.4426950408889634
GROUP_HEADS = 2
GROUP_ROWS = GROUP_HEADS * Q_BLOCK
NEG_BIG = -1e30
HALF_BIAS = 1 << 15
NEG_INF_KEY = -2 ** 31 + 0x007FFFFF
CONF_HALO = 32
SC_HALO = 8


def _dot(a, b):
    return jnp.dot(a, b, preferred_element_type=F32)


def _dot_nt(a, b):
    return lax.dot_general(a, b, (((1,), (1,)), ((), ())), preferred_element_type=F32)


def _rms(x, g):
    return x * lax.rsqrt(jnp.mean(x * x, axis=-1, keepdims=True) + NORM_EPS) * g


def _sigmoid(x):
    return 1.0 / (1.0 + jnp.exp(-x))


def _ffn_kernel(*refs, n_pre, has_bias, final_norm):
    it = iter(refs)
    x_ref = next(it)
    pre = [(next(it), next(it)) for _ in range(n_pre)]
    bias_ref = next(it) if has_bias else None
    g_ref, wg_ref, wu_ref, wd_ref = next(it), next(it), next(it), next(it)
    gf_ref = next(it) if final_norm else None
    o_ref = next(it)

    x = x_ref[...]
    if n_pre:
        upd = _dot(pre[0][0][...], pre[0][1][...])
        for a_ref, w_ref in pre[1:]:
            upd = upd + _dot(a_ref[...], w_ref[...])
        if has_bias:
            upd = upd + bias_ref[...]
        x = x + upd
    xn = _rms(x, g_ref[...]).astype(BF16)
    gate = _dot(xn, wg_ref[...])
    up = _dot(xn, wu_ref[...])
    mid = (gate * _sigmoid(gate) * up).astype(BF16)
    y = x + FFN_RES_WEIGHT * _dot(mid, wd_ref[...])
    if final_norm:
        y = _rms(y, gf_ref[...])
    o_ref[...] = y


def _const_spec(shape):
    nd = len(shape)
    return pl.BlockSpec(shape, lambda *_: (0,) * nd, pipeline_mode=pl.Buffered(1))


def _ffn(x2d, g, wg, wu, wd, pre=(), bias=None, final_g=None, tm=512):
    n = x2d.shape[0]
    tm = min(tm, n)
    row = lambda w: pl.BlockSpec((tm, w), lambda i: (i, 0))
    args, specs = [x2d], [row(D_MODEL)]
    for act, w in pre:
        args += [act, w]
        specs += [row(act.shape[1]), _const_spec(w.shape)]
    if bias is not None:
        args.append(bias)
        specs.append(_const_spec(bias.shape))
    args += [g, wg, wu, wd]
    specs += [_const_spec(g.shape), _const_spec(wg.shape), _const_spec(wu.shape), _const_spec(wd.shape)]
    if final_g is not None:
        args.append(final_g)
        specs.append(_const_spec(final_g.shape))
    kern = functools.partial(_ffn_kernel, n_pre=len(pre), has_bias=bias is not None,
                             final_norm=final_g is not None)
    return pl.pallas_call(
        kern, grid=(n // tm,), in_specs=specs, out_specs=row(D_MODEL),
        out_shape=jax.ShapeDtypeStruct((n, D_MODEL), F32),
        compiler_params=pltpu.CompilerParams(dimension_semantics=("arbitrary",),
                                             vmem_limit_bytes=VMEM_LIMIT),
        name="ffn",
    )(*args)


def _rope(z, cos, sin, half, period):
    width = z.shape[-1]
    lane = lax.broadcasted_iota(I32, z.shape, 1)
    first = (lane & (period - 1)) < half
    partner = jnp.where(first, pltpu.roll(z, width - half, 1), pltpu.roll(z, half, 1))
    reps = width // LANES
    if reps > 1:
        cos = jnp.concatenate([cos] * reps, axis=1)
        sin = jnp.concatenate([sin] * reps, axis=1)
    return z * cos + partner * sin


def _mix_in_kernel(x_ref, g_ref, wa_ref, wt_ref, wb_ref, cw_ref, tab_ref,
                   q_ref, qi_ref, k_ref, vt_ref, ki_ref, wit_ref, yc_ref,
                   cbuf_ref, carry_ref):
    s = pl.program_id(0)
    b = pl.program_id(1)
    tm = x_ref.shape[1]
    xn = _rms(x_ref[0], g_ref[...]).astype(BF16)
    za = _dot(xn, wa_ref[...])
    cos_a, sin_a = tab_ref[:, 0:LANES], tab_ref[:, LANES:2 * LANES]
    cos_i, sin_i = tab_ref[:, 2 * LANES:3 * LANES], tab_ref[:, 3 * LANES:4 * LANES]
    half_a = ATT_HEAD_DIM // ROT_FRACTION // 2
    half_i = IDX_DIM // ROT_FRACTION // 2

    q = _rope(za[:, A_Q:A_Q + ATT_WIDTH], cos_a, sin_a, half_a, ATT_HEAD_DIM)
    q_ref[0] = (q * (ATT_HEAD_DIM ** -0.5 * LOG2E)).astype(BF16)
    qi = _rope(za[:, A_QI:A_QI + IDX_WIDTH], cos_i, sin_i, half_i, IDX_DIM)
    qi_ref[0] = qi.astype(BF16)
    k = _rope(za[:, A_K:A_K + LANES], cos_a, sin_a, half_a, ATT_HEAD_DIM)
    k_ref[0] = k.astype(BF16)
    ki = _rope(za[:, A_KI:A_KI + LANES], cos_i, sin_i, half_i, IDX_DIM)
    ki_ref[0] = ki[:, :IDX_DIM].astype(BF16)
    zt = _dot_nt(wt_ref[...], xn)
    t_row = lax.broadcasted_iota(I32, zt.shape, 0)
    vt_ref[0, 0] = jnp.where(t_row < ATT_HEAD_DIM, zt,
                             jnp.where(t_row == ATT_HEAD_DIM, 1.0, 0.0)).astype(BF16)
    wit_ref[0] = zt[ATT_HEAD_DIM:, :] * ((IDX_HEADS * IDX_DIM) ** -0.5)

    zb = _dot(xn, wb_ref[...])
    gate_b = zb[:, 0:SC_WIDTH]
    cu = zb[:, SC_WIDTH:2 * SC_WIDTH] * zb[:, 2 * SC_WIDTH:3 * SC_WIDTH]

    @pl.when(s == 0)
    def _():
        cbuf_ref[0:SC_HALO, :] = jnp.zeros((SC_HALO, SC_WIDTH), F32)

    @pl.when(s > 0)
    def _():
        cbuf_ref[0:SC_HALO, :] = carry_ref[b]

    cbuf_ref[SC_HALO:SC_HALO + tm, :] = cu
    carry_ref[b] = cu[tm - SC_HALO:tm, :]
    conv = cu * cw_ref[SC_KERNEL - 1:SC_KERNEL, :]
    for j in range(SC_KERNEL - 1):
        off = SC_HALO - (SC_KERNEL - 1) + j
        conv = conv + cbuf_ref[off:off + tm, :] * cw_ref[j:j + 1, :]
    yc_ref[0] = (gate_b * conv).astype(BF16)


def _mix_in(h, g, wa, wt, wb, conv_w, tab):
    bsz, seq, _ = h.shape
    tm = KEY_CHUNK
    tok = lambda w: pl.BlockSpec((1, tm, w), lambda s, b: (b, s, 0))
    sds = jax.ShapeDtypeStruct
    out_shape = [sds((bsz, seq, ATT_WIDTH), BF16), sds((bsz, seq, IDX_WIDTH), BF16),
                 sds((bsz, seq, LANES), BF16), sds((bsz, seq // tm, VT_ROWS, tm), BF16),
                 sds((bsz, seq, IDX_DIM), BF16), sds((bsz, IDX_HEADS, seq), F32),
                 sds((bsz, seq, SC_WIDTH), BF16)]
    out_specs = [tok(ATT_WIDTH), tok(IDX_WIDTH), tok(LANES),
                 pl.BlockSpec((1, 1, VT_ROWS, tm), lambda s, b: (b, s, 0, 0)),
                 tok(IDX_DIM), pl.BlockSpec((1, IDX_HEADS, tm), lambda s, b: (b, 0, s)),
                 tok(SC_WIDTH)]
    return pl.pallas_call(
        _mix_in_kernel, grid=(seq // tm, bsz),
        in_specs=[tok(D_MODEL), _const_spec(g.shape), _const_spec(wa.shape), _const_spec(wt.shape),
                  _const_spec(wb.shape), _const_spec(conv_w.shape),
                  pl.BlockSpec((tm, 4 * LANES), lambda s, b: (s, 0))],
        out_specs=out_specs, out_shape=out_shape,
        scratch_shapes=[pltpu.VMEM((tm + SC_HALO, SC_WIDTH), F32),
                        pltpu.VMEM((bsz, SC_HALO, SC_WIDTH), F32)],
        compiler_params=pltpu.CompilerParams(dimension_semantics=("arbitrary", "arbitrary"),
                                             vmem_limit_bytes=VMEM_LIMIT),
        name="mix_in",
    )(h, g, wa, wt, wb, conv_w, tab)


def _sortable(x):
    bits = pltpu.bitcast(x + 0.0, I32)
    return bits ^ ((bits >> 31) & 0x7FFFFFFF)


def _count_ge(arr_ref, n_steps, group, cand):
    kc = arr_ref.shape[1]
    c16 = cand.astype(I16)
    one, zero = jnp.int16(1), jnp.int16(0)
    n_acc = 4

    def body(s, accs):
        accs = list(accs)
        for u in range(group):
            for r in range(kc // PACK_ROWS):
                t = arr_ref[s * group + u, r * PACK_ROWS:(r + 1) * PACK_ROWS, :]
                accs[r % n_acc] = accs[r % n_acc] + jnp.where(t >= c16, one, zero)
        return tuple(accs)

    accs = lax.fori_loop(0, n_steps, body,
                         tuple(jnp.zeros((PACK_ROWS, Q_BLOCK), I16) for _ in range(n_acc)))
    total = (accs[0] + accs[1]) + (accs[2] + accs[3])
    return jnp.sum(total.astype(F32), axis=0, keepdims=True)


def _descend(arr_ref, n_steps, group, base, kf, state, first_bit, n_bits):
    def step(t, carry):
        thr, cge, cgt = carry
        cand = thr | (jnp.int32(1) << (first_bit - t))
        n = base + _count_ge(arr_ref, n_steps, group, cand - HALF_BIAS)
        take = n >= kf
        return jnp.where(take, cand, thr), jnp.where(take, n, cge), jnp.where(take, cgt, n)

    return lax.fori_loop(0, n_bits, step, state)


def _dsa_kernel(q_ref, qi_ref, wit_ref, k_ref, vt_ref, ki_ref, o_ref,
                qs_ref, qis_ref, hi_ref, lo_ref, low_ref, bias_ref, thr_ref, need_ref, acc_ref,
                lga_ref, lgb_ref, *, topk, group):
    i = pl.program_id(1)
    kc = KEY_CHUNK
    rows = ATT_HEADS * Q_BLOCK
    n_chunks = (i * Q_BLOCK + Q_BLOCK + kc - 1) // kc
    n_pairs = (n_chunks + 1) // 2
    q_pos = i * Q_BLOCK + lax.broadcasted_iota(I32, (kc, Q_BLOCK), 1)
    kf = jnp.float32(topk)

    r_i = lax.broadcasted_iota(I32, (Q_BLOCK, Q_BLOCK), 0)
    c_i = lax.broadcasted_iota(I32, (Q_BLOCK, Q_BLOCK), 1)
    eye = jnp.where(r_i == c_i, 1.0, 0.0).astype(BF16)
    for h in range(ATT_HEADS):
        blk_rows = slice(h * Q_BLOCK, (h + 1) * Q_BLOCK)
        qs_ref[blk_rows, 0:LANES] = jnp.zeros((Q_BLOCK, LANES), BF16)
        qs_ref[blk_rows, 0:ATT_HEAD_DIM] = q_ref[0, :, h * ATT_HEAD_DIM:(h + 1) * ATT_HEAD_DIM]
        qs_ref[blk_rows, LANES:QS_COLS] = eye
    for h in range(IDX_HEADS):
        qis_ref[h * Q_BLOCK:(h + 1) * Q_BLOCK, :] = qi_ref[0, :, h * IDX_DIM:(h + 1) * IDX_DIM]

    wit = wit_ref[0]

    def score_chunk(c, diagonal):
        start = pl.multiple_of(c * kc, kc)
        ki_c = ki_ref[0, pl.ds(start, kc), :]
        score = jnp.zeros((kc, Q_BLOCK), F32)
        for g in range(IDX_HEADS // GROUP_HEADS):
            dots = _dot_nt(ki_c, qis_ref[g * GROUP_ROWS:(g + 1) * GROUP_ROWS, :])
            for j in range(GROUP_HEADS):
                h = g * GROUP_HEADS + j
                score = score + wit[h:h + 1, :] * jnp.maximum(dots[:, j * Q_BLOCK:(j + 1) * Q_BLOCK], 0.0)
        if diagonal:
            k_pos = start + lax.broadcasted_iota(I32, (kc, Q_BLOCK), 0)
            score = jnp.where(k_pos <= q_pos, score, -jnp.inf)
        key = _sortable(score)
        hi_ref[c] = (key >> 16).astype(I16)
        lo_ref[c] = ((key & 0xFFFF) - HALF_BIAS).astype(I16)

    def score_pair(p, carry):
        score_chunk(2 * p, False)
        score_chunk(2 * p + 1, False)
        return carry

    lax.fori_loop(0, n_pairs - 1, score_pair, 0)
    score_chunk(2 * n_pairs - 2, True)
    score_chunk(2 * n_pairs - 1, True)

    n_steps = (n_chunks + group - 1) // group

    def pad_chunk(c, carry):
        hi_ref[c] = jnp.full((kc, Q_BLOCK), NEG_INF_KEY >> 16, I16)
        lo_ref[c] = jnp.full((kc, Q_BLOCK), (NEG_INF_KEY & 0xFFFF) - HALF_BIAS, I16)
        return carry

    lax.fori_loop(2 * n_pairs, n_steps * group, pad_chunk, 0)

    @pl.when((i + 1) * Q_BLOCK <= topk)
    def _():
        above = NEG_INF_KEY + 1
        thr_ref[0] = jnp.full((1, Q_BLOCK), above >> 16, I32)
        thr_ref[1] = jnp.full((1, Q_BLOCK), (above & 0xFFFF) - HALF_BIAS, I32)
        need_ref[0] = jnp.zeros((1, Q_BLOCK), F32)
        need_ref[1] = jnp.zeros((1, Q_BLOCK), F32)

    @pl.when((i + 1) * Q_BLOCK > topk)
    def _():
        total = jnp.full((1, Q_BLOCK), n_steps * group * kc, I32).astype(F32)
        zero = jnp.zeros((1, Q_BLOCK), F32)
        zero_i = jnp.zeros((1, Q_BLOCK), I32)

        t_hi, cge_hi, cgt_hi = _descend(hi_ref, n_steps, group, zero, kf, (zero_i, total, zero), 15, 16)
        t_hi = t_hi - HALF_BIAS
        t_hi16 = t_hi.astype(I16)

        def bucket_step(s, carry):
            for u in range(group):
                c = s * group + u
                low_ref[c] = jnp.where(hi_ref[c] == t_hi16, lo_ref[c], jnp.int16(-HALF_BIAS))
            return carry

        lax.fori_loop(0, n_steps, bucket_step, 0)

        t_lo, cge, cgt = _descend(low_ref, n_steps, group, cgt_hi, kf, (zero_i, cge_hi, cgt_hi),
                                  15, LO_FIRST_BITS)
        thr_ref[1] = t_lo
        need_ref[0] = cge
        need_ref[1] = cgt

        @pl.when(jnp.max(jnp.abs(cge - kf)) > 0.0)
        def _():
            t, c_ge, c_gt = _descend(low_ref, n_steps, group, cgt_hi, kf,
                                     (thr_ref[1], need_ref[0], need_ref[1]),
                                     15 - LO_FIRST_BITS, 16 - LO_FIRST_BITS)
            thr_ref[1] = t
            need_ref[0] = c_ge
            need_ref[1] = c_gt

        cge, cgt = need_ref[0], need_ref[1]
        thr_ref[0] = t_hi
        thr_ref[1] = thr_ref[1] - HALF_BIAS
        need_ref[0] = kf - cgt
        need_ref[1] = cge - cgt

    t_hi16 = thr_ref[0].astype(I16)
    t_lo16 = thr_ref[1].astype(I16)

    def bias_pair(p, carry):
        for c in (2 * p, 2 * p + 1):
            hi, lo = hi_ref[c], lo_ref[c]
            sel = (hi > t_hi16) | ((hi == t_hi16) & (lo >= t_lo16))
            bias_ref[c] = jnp.where(sel, jnp.bfloat16(0.0), jnp.bfloat16(NEG_BIG))
        return carry

    lax.fori_loop(0, n_pairs, bias_pair, 0)

    @pl.when(jnp.max(need_ref[1] - need_ref[0]) > 0.0)
    def _():
        need = need_ref[0]
        r_i = lax.broadcasted_iota(I32, (kc, kc), 0)
        c_i = lax.broadcasted_iota(I32, (kc, kc), 1)
        tri = jnp.where(c_i <= r_i, 1.0, 0.0).astype(BF16)

        def tie_chunk(c, seen):
            eq = (hi_ref[c] == t_hi16) & (lo_ref[c] == t_lo16)
            eq_b = jnp.where(eq, jnp.bfloat16(1.0), jnp.bfloat16(0.0))
            rank = seen + _dot(tri, eq_b)
            drop = (eq_b.astype(F32) > 0.5) & (rank > need)
            bias_ref[c] = jnp.where(drop, NEG_BIG, bias_ref[c].astype(F32)).astype(BF16)
            return rank[kc - 1:kc, :]

        lax.fori_loop(0, n_chunks, tie_chunk, jnp.zeros((1, Q_BLOCK), F32))

    acc_ref[...] = jnp.zeros(acc_ref.shape, F32)

    n_groups = ATT_HEADS // GROUP_HEADS
    group = lambda g: slice(g * GROUP_ROWS, (g + 1) * GROUP_ROWS)

    def keys_of(c):
        start = pl.multiple_of(c * kc, kc)
        return jnp.concatenate([k_ref[0, pl.ds(start, kc), :], bias_ref[c]], axis=1)

    def softmax_step(lg_ref, g, v_c, m):
        lg = lg_ref[:, group(g)]
        m_new = jnp.maximum(m, jnp.max(lg, axis=0, keepdims=True))
        p = jnp.exp2(lg - m_new)
        acc_ref[:, group(g)] = jnp.exp2(m - m_new) * acc_ref[:, group(g)] + _dot(v_c, p.astype(BF16))
        return m_new

    last = 2 * n_pairs - 1
    lga_ref[...] = _dot_nt(keys_of(0), qs_ref[...])

    def half_step(src_ref, dst_ref, c_src, c_dst, ms):
        keys, v_c = keys_of(c_dst), vt_ref[0, c_src]
        out = []
        for g in range(n_groups):
            dst_ref[:, group(g)] = _dot_nt(keys, qs_ref[group(g), :])
            out.append(softmax_step(src_ref, g, v_c, ms[g]))
        return tuple(out)

    def attn_pair(p, ms):
        ms = half_step(lga_ref, lgb_ref, 2 * p, 2 * p + 1, ms)
        return half_step(lgb_ref, lga_ref, 2 * p + 1, jnp.minimum(2 * p + 2, last), ms)

    lax.fori_loop(0, n_pairs, attn_pair,
                  tuple(jnp.full((1, GROUP_ROWS), NEG_BIG, F32) for _ in range(n_groups)))

    denom = acc_ref[ATT_HEAD_DIM:ATT_HEAD_DIM + 1, :]
    out = (acc_ref[0:ATT_HEAD_DIM, :] / denom).astype(BF16)
    for h in range(ATT_HEADS):
        o_ref[0, :, h * ATT_HEAD_DIM:(h + 1) * ATT_HEAD_DIM] = _dot_nt(
            eye, out[:, h * Q_BLOCK:(h + 1) * Q_BLOCK]).astype(BF16)


def _dsa(q, qi, wi_t, k, v_t, ki):
    bsz, seq, _ = q.shape
    topk = min(TOPK_MAX, seq // 4)
    assert topk % Q_BLOCK == 0 and seq % (2 * KEY_CHUNK) == 0
    n_blk = seq // Q_BLOCK
    n_ck = seq // KEY_CHUNK
    group = SEARCH_GROUP if n_ck % SEARCH_GROUP == 0 else 2
    blk = lambda w: pl.BlockSpec((1, Q_BLOCK, w), lambda b, i: (b, i, 0))
    full = lambda w: pl.BlockSpec((1, seq, w), lambda b, i: (b, 0, 0))
    rows = ATT_HEADS * Q_BLOCK
    return pl.pallas_call(
        functools.partial(_dsa_kernel, topk=topk, group=group), grid=(bsz, n_blk),
        in_specs=[blk(ATT_WIDTH), blk(IDX_WIDTH),
                  pl.BlockSpec((1, IDX_HEADS, Q_BLOCK), lambda b, i: (b, 0, i)),
                  full(LANES),
                  pl.BlockSpec((1, n_ck, VT_ROWS, KEY_CHUNK), lambda b, i: (b, 0, 0, 0)),
                  full(IDX_DIM)],
        out_specs=blk(ATT_WIDTH),
        out_shape=jax.ShapeDtypeStruct((bsz, seq, ATT_WIDTH), BF16),
        scratch_shapes=[pltpu.VMEM((rows, QS_COLS), BF16),
                        pltpu.VMEM((IDX_HEADS * Q_BLOCK, IDX_DIM), BF16),
                        pltpu.VMEM((n_ck, KEY_CHUNK, Q_BLOCK), I16),
                        pltpu.VMEM((n_ck, KEY_CHUNK, Q_BLOCK), I16),
                        pltpu.VMEM((n_ck, KEY_CHUNK, Q_BLOCK), I16),
                        pltpu.VMEM((n_ck, KEY_CHUNK, Q_BLOCK), BF16),
                        pltpu.VMEM((2, 1, Q_BLOCK), I32),
                        pltpu.VMEM((2, 1, Q_BLOCK), F32),
                        pltpu.VMEM((VT_ROWS, rows), F32),
                        pltpu.VMEM((KEY_CHUNK, rows), F32),
                        pltpu.VMEM((KEY_CHUNK, rows), F32)],
        compiler_params=pltpu.CompilerParams(dimension_semantics=("arbitrary", "arbitrary"),
                                             vmem_limit_bytes=VMEM_LIMIT),
        name="dsa",
    )(q, qi, wi_t, k, v_t, ki)


def _conf_in_kernel(x_ref, g_ref, w1_ref, b1_ref, cw_ref, cb_ref, lg_ref, lb_ref, o_ref, ubuf_ref, sh_ref):
    s = pl.program_id(1)
    tm = x_ref.shape[1]
    xn = _rms(x_ref[0], g_ref[...]).astype(BF16)
    z = _dot(xn, w1_ref[...]) + b1_ref[...]
    u = z[:, :CONF_WIDTH] * _sigmoid(z[:, CONF_WIDTH:])

    @pl.when(s == 0)
    def _():
        ubuf_ref[0:CONF_HALO, :] = jnp.zeros((CONF_HALO, CONF_WIDTH), F32)

    @pl.when(s > 0)
    def _():
        ubuf_ref[0:CONF_HALO, :] = ubuf_ref[tm:tm + CONF_HALO, :]

    ubuf_ref[CONF_HALO:CONF_HALO + tm, :] = u
    span = tm + CONF_HALO - SUBLANES
    for r in range(1, SUBLANES):
        sh_ref[r - 1, 0:span, :] = ubuf_ref[r:r + span, :]
    conv = u * cw_ref[CONF_KERNEL - 1:CONF_KERNEL, :] + cb_ref[...]
    for j in range(CONF_KERNEL - 1):
        base, r = divmod(CONF_HALO - (CONF_KERNEL - 1) + j, SUBLANES)
        base *= SUBLANES
        win = ubuf_ref[base:base + tm, :] if r == 0 else sh_ref[r - 1, base:base + tm, :]
        conv = conv + win * cw_ref[j:j + 1, :]
    mu = jnp.mean(conv, axis=-1, keepdims=True)
    xc = conv - mu
    var = jnp.mean(xc * xc, axis=-1, keepdims=True)
    y = xc * lax.rsqrt(var + NORM_EPS) * lg_ref[...] + lb_ref[...]
    o_ref[0] = (y * _sigmoid(y)).astype(BF16)


def _conf_in(h, g, w1, b1, conv_w, conv_b, ln_g, ln_b, tm=256):
    bsz, seq, _ = h.shape
    tm = min(tm, seq)
    tok = lambda w: pl.BlockSpec((1, tm, w), lambda b, s: (b, s, 0))
    consts = (g, w1, b1, conv_w, conv_b, ln_g, ln_b)
    return pl.pallas_call(
        _conf_in_kernel, grid=(bsz, seq // tm),
        in_specs=[tok(D_MODEL)] + [_const_spec(c.shape) for c in consts],
        out_specs=tok(CONF_WIDTH),
        out_shape=jax.ShapeDtypeStruct((bsz, seq, CONF_WIDTH), BF16),
        scratch_shapes=[pltpu.VMEM((tm + CONF_HALO, CONF_WIDTH), F32),
                        pltpu.VMEM((SUBLANES - 1, tm + CONF_HALO - SUBLANES, CONF_WIDTH), F32)],
        compiler_params=pltpu.CompilerParams(dimension_semantics=("arbitrary", "arbitrary"),
                                             vmem_limit_bytes=VMEM_LIMIT),
        name="conf_in",
    )(h, *consts)


def _rope_table(seq, head_dim):
    half = head_dim // ROT_FRACTION // 2
    inv_freq = ROPE_THETA ** (-jnp.arange(half, dtype=F32) / half)
    ang = jnp.arange(seq, dtype=F32)[:, None] * inv_freq[None, :]
    cos, sin = jnp.cos(ang), jnp.sin(ang)
    rest = head_dim - 2 * half
    cos_h = jnp.concatenate([cos, cos, jnp.ones((seq, rest), F32)], axis=1)
    sin_h = jnp.concatenate([-sin, sin, jnp.zeros((seq, rest), F32)], axis=1)
    reps = LANES // head_dim
    return jnp.tile(cos_h, (1, reps)), jnp.tile(sin_h, (1, reps))


def _pack_w_in(w_in):
    sizes = (ATT_WIDTH, ATT_HEAD_DIM, ATT_HEAD_DIM, IDX_WIDTH, IDX_DIM, IDX_HEADS)
    offs = [0]
    for sz in sizes:
        offs.append(offs[-1] + sz)
    q, k, v, qi, ki, wi = (w_in[:, offs[j]:offs[j + 1]] for j in range(len(sizes)))
    pad = lambda w: jnp.pad(w, ((0, 0), (0, LANES - w.shape[1])))
    wa = jnp.concatenate([q, qi, pad(k), pad(ki)], axis=1)
    wt = jnp.concatenate([v, wi], axis=1).T
    return wa.astype(BF16), wt.astype(BF16), w_in[:, offs[-1]:].astype(BF16)


def kernel(x, ffn_norm, ffn_w_gate, ffn_w_up, ffn_w_down, mix_norm, hyb_w_in, hyb_conv_w, hyb_w_out,
           conf_w_pw1, conf_b_pw1, conf_conv_w, conf_conv_b, conf_ln_g, conf_ln_b, conf_w_pw2,
           conf_b_pw2, final_norm):
    bsz, seq, _ = x.shape
    n = bsz * seq
    depth = ffn_norm.shape[0]
    row = lambda a: a.reshape(1, -1)
    ffn_w = lambda layer, j: (row(ffn_norm[layer, j]), ffn_w_gate[layer, j].astype(BF16),
                              ffn_w_up[layer, j].astype(BF16), ffn_w_down[layer, j].astype(BF16))
    cos_a, sin_a = _rope_table(seq, ATT_HEAD_DIM)
    cos_i, sin_i = _rope_table(seq, IDX_DIM)
    tab = jnp.concatenate([cos_a, sin_a, cos_i, sin_i], axis=1)

    h = x.reshape(n, D_MODEL)
    for layer in range(depth):
        last = layer == depth - 1
        h = _ffn(h, *ffn_w(layer, 0))
        h3 = h.reshape(bsz, seq, D_MODEL)
        if layer % 2 == 0:
            e = layer // 2
            wa, wt, wb = _pack_w_in(hyb_w_in[e])
            q, qi, k, v_t, ki, wi_t, y_conv = _mix_in(h3, row(mix_norm[layer]), wa, wt, wb,
                                                      hyb_conv_w[e], tab)
            y_attn = _dsa(q, qi, wi_t, k, v_t, ki)
            w_out = hyb_w_out[e].astype(BF16)
            pre = ((y_attn.reshape(n, ATT_WIDTH), w_out[:ATT_WIDTH]),
                   (y_conv.reshape(n, SC_WIDTH), w_out[ATT_WIDTH:]))
            bias = None
        else:
            o = layer // 2
            u = _conf_in(h3, row(mix_norm[layer]), conf_w_pw1[o].astype(BF16), row(conf_b_pw1[o]),
                         conf_conv_w[o], row(conf_conv_b[o]), row(conf_ln_g[o]), row(conf_ln_b[o]))
            pre = ((u.reshape(n, CONF_WIDTH), conf_w_pw2[o].astype(BF16)),)
            bias = row(conf_b_pw2[o])
        h = _ffn(h, *ffn_w(layer, 1), pre=pre, bias=bias,
                 final_g=row(final_norm) if last else None)
    return h.reshape(bsz, seq, D_MODEL)
```

```python
import functools

import jax
import jax.numpy as jnp
from jax import lax
from jax.experimental import pallas as pl
from jax.experimental.pallas import tpu as pltpu

F32 = jnp.float32
BF16 = jnp.bfloat16
I32 = jnp.int32
I16 = jnp.int16

D_MODEL = 1024
D_FF = 2816
FFN_RES_WEIGHT = 0.5
ATT_HEADS = 8
ATT_HEAD_DIM = 64
ATT_WIDTH = ATT_HEADS * ATT_HEAD_DIM
IDX_HEADS = 8
IDX_DIM = 32
IDX_WIDTH = IDX_HEADS * IDX_DIM
TOPK_MAX = 256
Q_BLOCK = 128
SC_WIDTH = D_MODEL - ATT_WIDTH
SC_KERNEL = 3
CONF_WIDTH = D_MODEL
CONF_KERNEL = 31
ROPE_THETA = 500000.0
ROT_FRACTION = 4
NORM_EPS = 1e-6

LANES = 128
SUBLANES = 8
PACK_ROWS = 16
VMEM_LIMIT = 56 * 1024 * 1024

A_Q = 0
A_QI = A_Q + ATT_WIDTH
A_K = A_QI + IDX_WIDTH
A_KI = A_K + LANES
A_COLS = A_KI + LANES

KEY_CHUNK = 512
VT_ROWS = ATT_HEAD_DIM + IDX_HEADS
QS_COLS = 2 * LANES
SEARCH_GROUP = 1
LO_FIRST_BITS = 11
LOG2E = 1.4426950408889634
GROUP_HEADS = 2
GROUP_ROWS = GROUP_HEADS * Q_BLOCK
NEG_BIG = -1e30
HALF_BIAS = 1 << 15
NEG_INF_KEY = -2 ** 31 + 0x007FFFFF
CONF_HALO = 32
SC_HALO = 8


def _dot(a, b):
    return jnp.dot(a, b, preferred_element_type=F32)


def _dot_nt(a, b):
    return lax.dot_general(a, b, (((1,), (1,)), ((), ())), preferred_element_type=F32)


def _rms(x, g):
    return x * lax.rsqrt(jnp.mean(x * x, axis=-1, keepdims=True) + NORM_EPS) * g


def _sigmoid(x):
    return 1.0 / (1.0 + jnp.exp(-x))


def _ffn_kernel(*refs, n_pre, has_bias, final_norm):
    it = iter(refs)
    x_ref = next(it)
    pre = [(next(it), next(it)) for _ in range(n_pre)]
    bias_ref = next(it) if has_bias else None
    g_ref, wg_ref, wu_ref, wd_ref = next(it), next(it), next(it), next(it)
    gf_ref = next(it) if final_norm else None
    o_ref = next(it)

    x = x_ref[...]
    if n_pre:
        upd = _dot(pre[0][0][...], pre[0][1][...])
        for a_ref, w_ref in pre[1:]:
            upd = upd + _dot(a_ref[...], w_ref[...])
        if has_bias:
            upd = upd + bias_ref[...]
        x = x + upd
    xn = _rms(x, g_ref[...]).astype(BF16)
    gate = _dot(xn, wg_ref[...])
    up = _dot(xn, wu_ref[...])
    mid = (gate * _sigmoid(gate) * up).astype(BF16)
    y = x + FFN_RES_WEIGHT * _dot(mid, wd_ref[...])
    if final_norm:
        y = _rms(y, gf_ref[...])
    o_ref[...] = y


def _const_spec(shape):
    nd = len(shape)
    return pl.BlockSpec(shape, lambda *_: (0,) * nd, pipeline_mode=pl.Buffered(1))


def _ffn(x2d, g, wg, wu, wd, pre=(), bias=None, final_g=None, tm=512):
    n = x2d.shape[0]
    tm = min(tm, n)
    row = lambda w: pl.BlockSpec((tm, w), lambda i: (i, 0))
    args, specs = [x2d], [row(D_MODEL)]
    for act, w in pre:
        args += [act, w]
        specs += [row(act.shape[1]), _const_spec(w.shape)]
    if bias is not None:
        args.append(bias)
        specs.append(_const_spec(bias.shape))
    args += [g, wg, wu, wd]
    specs += [_const_spec(g.shape), _const_spec(wg.shape), _const_spec(wu.shape), _const_spec(wd.shape)]
    if final_g is not None:
        args.append(final_g)
        specs.append(_const_spec(final_g.shape))
    kern = functools.partial(_ffn_kernel, n_pre=len(pre), has_bias=bias is not None,
                             final_norm=final_g is not None)
    return pl.pallas_call(
        kern, grid=(n // tm,), in_specs=specs, out_specs=row(D_MODEL),
        out_shape=jax.ShapeDtypeStruct((n, D_MODEL), F32),
        compiler_params=pltpu.CompilerParams(dimension_semantics=("arbitrary",),
                                             vmem_limit_bytes=VMEM_LIMIT),
        name="ffn",
    )(*args)


def _rope(z, cos, sin, half, period):
    width = z.shape[-1]
    lane = lax.broadcasted_iota(I32, z.shape, 1)
    first = (lane & (period - 1)) < half
    partner = jnp.where(first, pltpu.roll(z, width - half, 1), pltpu.roll(z, half, 1))
    reps = width // LANES
    if reps > 1:
        cos = jnp.concatenate([cos] * reps, axis=1)
        sin = jnp.concatenate([sin] * reps, axis=1)
    return z * cos + partner * sin


def _mix_in_kernel(x_ref, g_ref, wa_ref, wt_ref, wb_ref, cw_ref, tab_ref,
                   q_ref, qi_ref, k_ref, vt_ref, ki_ref, wit_ref, yc_ref,
                   cbuf_ref, carry_ref):
    s = pl.program_id(0)
    b = pl.program_id(1)
    tm = x_ref.shape[1]
    xn = _rms(x_ref[0], g_ref[...]).astype(BF16)
    za = _dot(xn, wa_ref[...])
    cos_a, sin_a = tab_ref[:, 0:LANES], tab_ref[:, LANES:2 * LANES]
    cos_i, sin_i = tab_ref[:, 2 * LANES:3 * LANES], tab_ref[:, 3 * LANES:4 * LANES]
    half_a = ATT_HEAD_DIM // ROT_FRACTION // 2
    half_i = IDX_DIM // ROT_FRACTION // 2

    q = _rope(za[:, A_Q:A_Q + ATT_WIDTH], cos_a, sin_a, half_a, ATT_HEAD_DIM)
    q_ref[0] = (q * (ATT_HEAD_DIM ** -0.5 * LOG2E)).astype(BF16)
    qi = _rope(za[:, A_QI:A_QI + IDX_WIDTH], cos_i, sin_i, half_i, IDX_DIM)
    qi_ref[0] = qi.astype(BF16)
    k = _rope(za[:, A_K:A_K + LANES], cos_a, sin_a, half_a, ATT_HEAD_DIM)
    k_ref[0] = k.astype(BF16)
    ki = _rope(za[:, A_KI:A_KI + LANES], cos_i, sin_i, half_i, IDX_DIM)
    ki_ref[0] = ki[:, :IDX_DIM].astype(BF16)
    zt = _dot_nt(wt_ref[...], xn)
    t_row = lax.broadcasted_iota(I32, zt.shape, 0)
    vt_ref[0, 0] = jnp.where(t_row < ATT_HEAD_DIM, zt,
                             jnp.where(t_row == ATT_HEAD_DIM, 1.0, 0.0)).astype(BF16)
    wit_ref[0] = zt[ATT_HEAD_DIM:, :] * ((IDX_HEADS * IDX_DIM) ** -0.5)

    zb = _dot(xn, wb_ref[...])
    gate_b = zb[:, 0:SC_WIDTH]
    cu = zb[:, SC_WIDTH:2 * SC_WIDTH] * zb[:, 2 * SC_WIDTH:3 * SC_WIDTH]

    @pl.when(s == 0)
    def _():
        cbuf_ref[0:SC_HALO, :] = jnp.zeros((SC_HALO, SC_WIDTH), F32)

    @pl.when(s > 0)
    def _():
        cbuf_ref[0:SC_HALO, :] = carry_ref[b]

    cbuf_ref[SC_HALO:SC_HALO + tm, :] = cu
    carry_ref[b] = cu[tm - SC_HALO:tm, :]
    conv = cu * cw_ref[SC_KERNEL - 1:SC_KERNEL, :]
    for j in range(SC_KERNEL - 1):
        off = SC_HALO - (SC_KERNEL - 1) + j
        conv = conv + cbuf_ref[off:off + tm, :] * cw_ref[j:j + 1, :]
    yc_ref[0] = (gate_b * conv).astype(BF16)


def _mix_in(h, g, wa, wt, wb, conv_w, tab):
    bsz, seq, _ = h.shape
    tm = KEY_CHUNK
    tok = lambda w: pl.BlockSpec((1, tm, w), lambda s, b: (b, s, 0))
    sds = jax.ShapeDtypeStruct
    out_shape = [sds((bsz, seq, ATT_WIDTH), BF16), sds((bsz, seq, IDX_WIDTH), BF16),
                 sds((bsz, seq, LANES), BF16), sds((bsz, seq // tm, VT_ROWS, tm), BF16),
                 sds((bsz, seq, IDX_DIM), BF16), sds((bsz, IDX_HEADS, seq), F32),
                 sds((bsz, seq, SC_WIDTH), BF16)]
    out_specs = [tok(ATT_WIDTH), tok(IDX_WIDTH), tok(LANES),
                 pl.BlockSpec((1, 1, VT_ROWS, tm), lambda s, b: (b, s, 0, 0)),
                 tok(IDX_DIM), pl.BlockSpec((1, IDX_HEADS, tm), lambda s, b: (b, 0, s)),
                 tok(SC_WIDTH)]
    return pl.pallas_call(
        _mix_in_kernel, grid=(seq // tm, bsz),
        in_specs=[tok(D_MODEL), _const_spec(g.shape), _const_spec(wa.shape), _const_spec(wt.shape),
                  _const_spec(wb.shape), _const_spec(conv_w.shape),
                  pl.BlockSpec((tm, 4 * LANES), lambda s, b: (s, 0))],
        out_specs=out_specs, out_shape=out_shape,
        scratch_shapes=[pltpu.VMEM((tm + SC_HALO, SC_WIDTH), F32),
                        pltpu.VMEM((bsz, SC_HALO, SC_WIDTH), F32)],
        compiler_params=pltpu.CompilerParams(dimension_semantics=("arbitrary", "arbitrary"),
                                             vmem_limit_bytes=VMEM_LIMIT),
        name="mix_in",
    )(h, g, wa, wt, wb, conv_w, tab)


def _sortable(x):
    bits = pltpu.bitcast(x + 0.0, I32)
    return bits ^ ((bits >> 31) & 0x7FFFFFFF)


def _count_ge(arr_ref, n_steps, group, cand):
    kc = arr_ref.shape[1]
    c16 = cand.astype(I16)
    one, zero = jnp.int16(1), jnp.int16(0)
    n_acc = 4

    def body(s, accs):
        accs = list(accs)
        for u in range(group):
            for r in range(kc // PACK_ROWS):
                t = arr_ref[s * group + u, r * PACK_ROWS:(r + 1) * PACK_ROWS, :]
                accs[r % n_acc] = accs[r % n_acc] + jnp.where(t >= c16, one, zero)
        return tuple(accs)

    accs = lax.fori_loop(0, n_steps, body,
                         tuple(jnp.zeros((PACK_ROWS, Q_BLOCK), I16) for _ in range(n_acc)))
    total = (accs[0] + accs[1]) + (accs[2] + accs[3])
    return jnp.sum(total.astype(F32), axis=0, keepdims=True)


def _descend(arr_ref, n_steps, group, base, kf, state, first_bit, n_bits):
    def step(t, carry):
        thr, cge, cgt = carry
        cand = thr | (jnp.int32(1) << (first_bit - t))
        n = base + _count_ge(arr_ref, n_steps, group, cand - HALF_BIAS)
        take = n >= kf
        return jnp.where(take, cand, thr), jnp.where(take, n, cge), jnp.where(take, cgt, n)

    return lax.fori_loop(0, n_bits, step, state)


def _dsa_kernel(q_ref, qi_ref, wit_ref, k_ref, vt_ref, ki_ref, o_ref,
                qs_ref, qis_ref, hi_ref, lo_ref, low_ref, bias_ref, thr_ref, need_ref, acc_ref,
                lga_ref, lgb_ref, *, topk, group):
    i = pl.program_id(1)
    kc = KEY_CHUNK
    rows = ATT_HEADS * Q_BLOCK
    n_chunks = (i * Q_BLOCK + Q_BLOCK + kc - 1) // kc
    n_pairs = (n_chunks + 1) // 2
    q_pos = i * Q_BLOCK + lax.broadcasted_iota(I32, (kc, Q_BLOCK), 1)
    kf = jnp.float32(topk)

    r_i = lax.broadcasted_iota(I32, (Q_BLOCK, Q_BLOCK), 0)
    c_i = lax.broadcasted_iota(I32, (Q_BLOCK, Q_BLOCK), 1)
    eye = jnp.where(r_i == c_i, 1.0, 0.0).astype(BF16)
    for h in range(ATT_HEADS):
        blk_rows = slice(h * Q_BLOCK, (h + 1) * Q_BLOCK)
        qs_ref[blk_rows, 0:LANES] = jnp.zeros((Q_BLOCK, LANES), BF16)
        qs_ref[blk_rows, 0:ATT_HEAD_DIM] = q_ref[0, :, h * ATT_HEAD_DIM:(h + 1) * ATT_HEAD_DIM]
        qs_ref[blk_rows, LANES:QS_COLS] = eye
    for h in range(IDX_HEADS):
        qis_ref[h * Q_BLOCK:(h + 1) * Q_BLOCK, :] = qi_ref[0, :, h * IDX_DIM:(h + 1) * IDX_DIM]

    wit = wit_ref[0]

    def score_chunk(c, diagonal):
        start = pl.multiple_of(c * kc, kc)
        ki_c = ki_ref[0, pl.ds(start, kc), :]
        score = jnp.zeros((kc, Q_BLOCK), F32)
        for g in range(IDX_HEADS // GROUP_HEADS):
            dots = _dot_nt(ki_c, qis_ref[g * GROUP_ROWS:(g + 1) * GROUP_ROWS, :])
            for j in range(GROUP_HEADS):
                h = g * GROUP_HEADS + j
                score = score + wit[h:h + 1, :] * jnp.maximum(dots[:, j * Q_BLOCK:(j + 1) * Q_BLOCK], 0.0)
        if diagonal:
            k_pos = start + lax.broadcasted_iota(I32, (kc, Q_BLOCK), 0)
            score = jnp.where(k_pos <= q_pos, score, -jnp.inf)
        key = _sortable(score)
        hi_ref[c] = (key >> 16).astype(I16)
        lo_ref[c] = ((key & 0xFFFF) - HALF_BIAS).astype(I16)

    def score_pair(p, carry):
        score_chunk(2 * p, False)
        score_chunk(2 * p + 1, False)
        return carry

    lax.fori_loop(0, n_pairs - 1, score_pair, 0)
    score_chunk(2 * n_pairs - 2, True)

    @pl.when(n_chunks % 2 == 0)
    def _():
        score_chunk(2 * n_pairs - 1, True)

    n_steps = (n_chunks + group - 1) // group

    def pad_chunk(c, carry):
        hi_ref[c] = jnp.full((kc, Q_BLOCK), NEG_INF_KEY >> 16, I16)
        lo_ref[c] = jnp.full((kc, Q_BLOCK), (NEG_INF_KEY & 0xFFFF) - HALF_BIAS, I16)
        return carry

    lax.fori_loop(n_chunks, n_steps * group, pad_chunk, 0)

    @pl.when((i + 1) * Q_BLOCK <= topk)
    def _():
        above = NEG_INF_KEY + 1
        thr_ref[0] = jnp.full((1, Q_BLOCK), above >> 16, I32)
        thr_ref[1] = jnp.full((1, Q_BLOCK), (above & 0xFFFF) - HALF_BIAS, I32)
        need_ref[0] = jnp.zeros((1, Q_BLOCK), F32)
        need_ref[1] = jnp.zeros((1, Q_BLOCK), F32)

    @pl.when((i + 1) * Q_BLOCK > topk)
    def _():
        total = jnp.full((1, Q_BLOCK), n_steps * group * kc, I32).astype(F32)
        zero = jnp.zeros((1, Q_BLOCK), F32)
        zero_i = jnp.zeros((1, Q_BLOCK), I32)

        t_hi, cge_hi, cgt_hi = _descend(hi_ref, n_steps, group, zero, kf, (zero_i, total, zero), 15, 16)
        t_hi = t_hi - HALF_BIAS
        t_hi16 = t_hi.astype(I16)

        def bucket_step(s, carry):
            for u in range(group):
                c = s * group + u
                low_ref[c] = jnp.where(hi_ref[c] == t_hi16, lo_ref[c], jnp.int16(-HALF_BIAS))
            return carry

        lax.fori_loop(0, n_steps, bucket_step, 0)

        t_lo, cge, cgt = _descend(low_ref, n_steps, group, cgt_hi, kf, (zero_i, cge_hi, cgt_hi),
                                  15, LO_FIRST_BITS)
        thr_ref[1] = t_lo
        need_ref[0] = cge
        need_ref[1] = cgt

        @pl.when(jnp.max(jnp.abs(cge - kf)) > 0.0)
        def _():
            t, c_ge, c_gt = _descend(low_ref, n_steps, group, cgt_hi, kf,
                                     (thr_ref[1], need_ref[0], need_ref[1]),
                                     15 - LO_FIRST_BITS, 16 - LO_FIRST_BITS)
            thr_ref[1] = t
            need_ref[0] = c_ge
            need_ref[1] = c_gt

        cge, cgt = need_ref[0], need_ref[1]
        thr_ref[0] = t_hi
        thr_ref[1] = thr_ref[1] - HALF_BIAS
        need_ref[0] = kf - cgt
        need_ref[1] = cge - cgt

    t_hi16 = thr_ref[0].astype(I16)
    t_lo16 = thr_ref[1].astype(I16)

    def bias_chunk(c, carry):
        hi, lo = hi_ref[c], lo_ref[c]
        sel = (hi > t_hi16) | ((hi == t_hi16) & (lo >= t_lo16))
        bias_ref[c] = jnp.where(sel, jnp.bfloat16(0.0), jnp.bfloat16(NEG_BIG))
        return carry

    lax.fori_loop(0, n_chunks, bias_chunk, 0)

    @pl.when(jnp.max(need_ref[1] - need_ref[0]) > 0.0)
    def _():
        need = need_ref[0]
        r_i = lax.broadcasted_iota(I32, (kc, kc), 0)
        c_i = lax.broadcasted_iota(I32, (kc, kc), 1)
        tri = jnp.where(c_i <= r_i, 1.0, 0.0).astype(BF16)

        def tie_chunk(c, seen):
            eq = (hi_ref[c] == t_hi16) & (lo_ref[c] == t_lo16)
            eq_b = jnp.where(eq, jnp.bfloat16(1.0), jnp.bfloat16(0.0))
            rank = seen + _dot(tri, eq_b)
            drop = (eq_b.astype(F32) > 0.5) & (rank > need)
            bias_ref[c] = jnp.where(drop, NEG_BIG, bias_ref[c].astype(F32)).astype(BF16)
            return rank[kc - 1:kc, :]

        lax.fori_loop(0, n_chunks, tie_chunk, jnp.zeros((1, Q_BLOCK), F32))

    acc_ref[...] = jnp.zeros(acc_ref.shape, F32)

    n_groups = ATT_HEADS // GROUP_HEADS
    group = lambda g: slice(g * GROUP_ROWS, (g + 1) * GROUP_ROWS)

    def keys_of(c):
        start = pl.multiple_of(c * kc, kc)
        return jnp.concatenate([k_ref[0, pl.ds(start, kc), :], bias_ref[c]], axis=1)

    def softmax_step(lg_ref, g, v_c, m):
        lg = lg_ref[:, group(g)]
        m_new = jnp.maximum(m, jnp.max(lg, axis=0, keepdims=True))
        p = jnp.exp2(lg - m_new)
        acc_ref[:, group(g)] = jnp.exp2(m - m_new) * acc_ref[:, group(g)] + _dot(v_c, p.astype(BF16))
        return m_new

    last = n_chunks - 1
    lga_ref[...] = _dot_nt(keys_of(0), qs_ref[...])

    def half_step(src_ref, dst_ref, c_src, c_dst, ms):
        keys, v_c = keys_of(c_dst), vt_ref[0, c_src]
        out = []
        for g in range(n_groups):
            dst_ref[:, group(g)] = _dot_nt(keys, qs_ref[group(g), :])
            out.append(softmax_step(src_ref, g, v_c, ms[g]))
        return tuple(out)

    def attn_pair(p, ms):
        ms = half_step(lga_ref, lgb_ref, 2 * p, 2 * p + 1, ms)
        return half_step(lgb_ref, lga_ref, 2 * p + 1, jnp.minimum(2 * p + 2, last), ms)

    ms = lax.fori_loop(0, n_chunks // 2, attn_pair,
                       tuple(jnp.full((1, GROUP_ROWS), NEG_BIG, F32) for _ in range(n_groups)))

    @pl.when(n_chunks % 2 == 1)
    def _():
        v_c = vt_ref[0, last]
        for g in range(n_groups):
            softmax_step(lga_ref, g, v_c, ms[g])

    denom = acc_ref[ATT_HEAD_DIM:ATT_HEAD_DIM + 1, :]
    out = (acc_ref[0:ATT_HEAD_DIM, :] / denom).astype(BF16)
    for h in range(ATT_HEADS):
        o_ref[0, :, h * ATT_HEAD_DIM:(h + 1) * ATT_HEAD_DIM] = _dot_nt(
            eye, out[:, h * Q_BLOCK:(h + 1) * Q_BLOCK]).astype(BF16)


def _dsa(q, qi, wi_t, k, v_t, ki):
    bsz, seq, _ = q.shape
    topk = min(TOPK_MAX, seq // 4)
    assert topk % Q_BLOCK == 0 and seq % (2 * KEY_CHUNK) == 0
    n_blk = seq // Q_BLOCK
    n_ck = seq // KEY_CHUNK
    group = SEARCH_GROUP if n_ck % SEARCH_GROUP == 0 else 2
    blk = lambda w: pl.BlockSpec((1, Q_BLOCK, w), lambda b, i: (b, i, 0))
    full = lambda w: pl.BlockSpec((1, seq, w), lambda b, i: (b, 0, 0))
    rows = ATT_HEADS * Q_BLOCK
    return pl.pallas_call(
        functools.partial(_dsa_kernel, topk=topk, group=group), grid=(bsz, n_blk),
        in_specs=[blk(ATT_WIDTH), blk(IDX_WIDTH),
                  pl.BlockSpec((1, IDX_HEADS, Q_BLOCK), lambda b, i: (b, 0, i)),
                  full(LANES),
                  pl.BlockSpec((1, n_ck, VT_ROWS, KEY_CHUNK), lambda b, i: (b, 0, 0, 0)),
                  full(IDX_DIM)],
        out_specs=blk(ATT_WIDTH),
        out_shape=jax.ShapeDtypeStruct((bsz, seq, ATT_WIDTH), BF16),
        scratch_shapes=[pltpu.VMEM((rows, QS_COLS), BF16),
                        pltpu.VMEM((IDX_HEADS * Q_BLOCK, IDX_DIM), BF16),
                        pltpu.VMEM((n_ck, KEY_CHUNK, Q_BLOCK), I16),
                        pltpu.VMEM((n_ck, KEY_CHUNK, Q_BLOCK), I16),
                        pltpu.VMEM((n_ck, KEY_CHUNK, Q_BLOCK), I16),
                        pltpu.VMEM((n_ck, KEY_CHUNK, Q_BLOCK), BF16),
                        pltpu.VMEM((2, 1, Q_BLOCK), I32),
                        pltpu.VMEM((2, 1, Q_BLOCK), F32),
                        pltpu.VMEM((VT_ROWS, rows), F32),
                        pltpu.VMEM((KEY_CHUNK, rows), F32),
                        pltpu.VMEM((KEY_CHUNK, rows), F32)],
        compiler_params=pltpu.CompilerParams(dimension_semantics=("arbitrary", "arbitrary"),
                                             vmem_limit_bytes=VMEM_LIMIT),
        name="dsa",
    )(q, qi, wi_t, k, v_t, ki)


def _conf_in_kernel(x_ref, g_ref, w1_ref, b1_ref, cw_ref, cb_ref, lg_ref, lb_ref, o_ref, ubuf_ref, sh_ref):
    s = pl.program_id(1)
    tm = x_ref.shape[1]
    xn = _rms(x_ref[0], g_ref[...]).astype(BF16)
    z = _dot(xn, w1_ref[...]) + b1_ref[...]
    u = z[:, :CONF_WIDTH] * _sigmoid(z[:, CONF_WIDTH:])

    @pl.when(s == 0)
    def _():
        ubuf_ref[0:CONF_HALO, :] = jnp.zeros((CONF_HALO, CONF_WIDTH), F32)

    @pl.when(s > 0)
    def _():
        ubuf_ref[0:CONF_HALO, :] = ubuf_ref[tm:tm + CONF_HALO, :]

    ubuf_ref[CONF_HALO:CONF_HALO + tm, :] = u
    span = tm + CONF_HALO - SUBLANES
    for r in range(1, SUBLANES):
        sh_ref[r - 1, 0:span, :] = ubuf_ref[r:r + span, :]
    conv = u * cw_ref[CONF_KERNEL - 1:CONF_KERNEL, :] + cb_ref[...]
    for j in range(CONF_KERNEL - 1):
        base, r = divmod(CONF_HALO - (CONF_KERNEL - 1) + j, SUBLANES)
        base *= SUBLANES
        win = ubuf_ref[base:base + tm, :] if r == 0 else sh_ref[r - 1, base:base + tm, :]
        conv = conv + win * cw_ref[j:j + 1, :]
    mu = jnp.mean(conv, axis=-1, keepdims=True)
    xc = conv - mu
    var = jnp.mean(xc * xc, axis=-1, keepdims=True)
    y = xc * lax.rsqrt(var + NORM_EPS) * lg_ref[...] + lb_ref[...]
    o_ref[0] = (y * _sigmoid(y)).astype(BF16)


def _conf_in(h, g, w1, b1, conv_w, conv_b, ln_g, ln_b, tm=256):
    bsz, seq, _ = h.shape
    tm = min(tm, seq)
    tok = lambda w: pl.BlockSpec((1, tm, w), lambda b, s: (b, s, 0))
    consts = (g, w1, b1, conv_w, conv_b, ln_g, ln_b)
    return pl.pallas_call(
        _conf_in_kernel, grid=(bsz, seq // tm),
        in_specs=[tok(D_MODEL)] + [_const_spec(c.shape) for c in consts],
        out_specs=tok(CONF_WIDTH),
        out_shape=jax.ShapeDtypeStruct((bsz, seq, CONF_WIDTH), BF16),
        scratch_shapes=[pltpu.VMEM((tm + CONF_HALO, CONF_WIDTH), F32),
                        pltpu.VMEM((SUBLANES - 1, tm + CONF_HALO - SUBLANES, CONF_WIDTH), F32)],
        compiler_params=pltpu.CompilerParams(dimension_semantics=("arbitrary", "arbitrary"),
                                             vmem_limit_bytes=VMEM_LIMIT),
        name="conf_in",
    )(h, *consts)


def _rope_table(seq, head_dim):
    half = head_dim // ROT_FRACTION // 2
    inv_freq = ROPE_THETA ** (-jnp.arange(half, dtype=F32) / half)
    ang = jnp.arange(seq, dtype=F32)[:, None] * inv_freq[None, :]
    cos, sin = jnp.cos(ang), jnp.sin(ang)
    rest = head_dim - 2 * half
    cos_h = jnp.concatenate([cos, cos, jnp.ones((seq, rest), F32)], axis=1)
    sin_h = jnp.concatenate([-sin, sin, jnp.zeros((seq, rest), F32)], axis=1)
    reps = LANES // head_dim
    return jnp.tile(cos_h, (1, reps)), jnp.tile(sin_h, (1, reps))


def _pack_w_in(w_in):
    sizes = (ATT_WIDTH, ATT_HEAD_DIM, ATT_HEAD_DIM, IDX_WIDTH, IDX_DIM, IDX_HEADS)
    offs = [0]
    for sz in sizes:
        offs.append(offs[-1] + sz)
    q, k, v, qi, ki, wi = (w_in[:, offs[j]:offs[j + 1]] for j in range(len(sizes)))
    pad = lambda w: jnp.pad(w, ((0, 0), (0, LANES - w.shape[1])))
    wa = jnp.concatenate([q, qi, pad(k), pad(ki)], axis=1)
    wt = jnp.concatenate([v, wi], axis=1).T
    return wa.astype(BF16), wt.astype(BF16), w_in[:, offs[-1]:].astype(BF16)


def kernel(x, ffn_norm, ffn_w_gate, ffn_w_up, ffn_w_down, mix_norm, hyb_w_in, hyb_conv_w, hyb_w_out,
           conf_w_pw1, conf_b_pw1, conf_conv_w, conf_conv_b, conf_ln_g, conf_ln_b, conf_w_pw2,
           conf_b_pw2, final_norm):
    bsz, seq, _ = x.shape
    n = bsz * seq
    depth = ffn_norm.shape[0]
    row = lambda a: a.reshape(1, -1)
    ffn_w = lambda layer, j: (row(ffn_norm[layer, j]), ffn_w_gate[layer, j].astype(BF16),
                              ffn_w_up[layer, j].astype(BF16), ffn_w_down[layer, j].astype(BF16))
    cos_a, sin_a = _rope_table(seq, ATT_HEAD_DIM)
    cos_i, sin_i = _rope_table(seq, IDX_DIM)
    tab = jnp.concatenate([cos_a, sin_a, cos_i, sin_i], axis=1)

    h = x.reshape(n, D_MODEL)
    for layer in range(depth):
        last = layer == depth - 1
        h = _ffn(h, *ffn_w(layer, 0))
        h3 = h.reshape(bsz, seq, D_MODEL)
        if layer % 2 == 0:
            e = layer // 2
            wa, wt, wb = _pack_w_in(hyb_w_in[e])
            q, qi, k, v_t, ki, wi_t, y_conv = _mix_in(h3, row(mix_norm[layer]), wa, wt, wb,
                                                      hyb_conv_w[e], tab)
            y_attn = _dsa(q, qi, wi_t, k, v_t, ki)
            w_out = hyb_w_out[e].astype(BF16)
            pre = ((y_attn.reshape(n, ATT_WIDTH), w_out[:ATT_WIDTH]),
                   (y_conv.reshape(n, SC_WIDTH), w_out[ATT_WIDTH:]))
            bias = None
        else:
            o = layer // 2
            u = _conf_in(h3, row(mix_norm[layer]), conf_w_pw1[o].astype(BF16), row(conf_b_pw1[o]),
                         conf_conv_w[o], row(conf_conv_b[o]), row(conf_ln_g[o]), row(conf_ln_b[o]))
            pre = ((u.reshape(n, CONF_WIDTH), conf_w_pw2[o].astype(BF16)),)
            bias = row(conf_b_pw2[o])
        h = _ffn(h, *ffn_w(layer, 1), pre=pre, bias=bias,
                 final_g=row(final_norm) if last else None)
    return h.reshape(bsz, seq, D_MODEL)
```

```python
import functools

import jax
import jax.numpy as jnp
from jax import lax
from jax.experimental import pallas as pl
from jax.experimental.pallas import tpu as pltpu

F32 = jnp.float32
BF16 = jnp.bfloat16
I32 = jnp.int32
I16 = jnp.int16

D_MODEL = 1024
D_FF = 2816
FFN_RES_WEIGHT = 0.5
ATT_HEADS = 8
ATT_HEAD_DIM = 64
ATT_WIDTH = ATT_HEADS * ATT_HEAD_DIM
IDX_HEADS = 8
IDX_DIM = 32
IDX_WIDTH = IDX_HEADS * IDX_DIM
TOPK_MAX = 256
Q_BLOCK = 128
SC_WIDTH = D_MODEL - ATT_WIDTH
SC_KERNEL = 3
CONF_WIDTH = D_MODEL
CONF_KERNEL = 31
ROPE_THETA = 500000.0
ROT_FRACTION = 4
NORM_EPS = 1e-6

LANES = 128
SUBLANES = 8
PACK_ROWS = 16
VMEM_LIMIT = 56 * 1024 * 1024

A_Q = 0
A_QI = A_Q + ATT_WIDTH
A_K = A_QI + IDX_WIDTH
A_KI = A_K + LANES
A_COLS = A_KI + LANES

KEY_CHUNK = 512
VT_ROWS = ATT_HEAD_DIM + IDX_HEADS
QS_COLS = 2 * LANES
SEARCH_GROUP = 1
LO_FIRST_BITS = 11
LOG2E = 1.4426950408889634
GROUP_HEADS = 2
GROUP_ROWS = GROUP_HEADS * Q_BLOCK
NEG_BIG = -1e30
HALF_BIAS = 1 << 15
NEG_INF_KEY = -2 ** 31 + 0x007FFFFF
CONF_HALO = 32
SC_HALO = 8


def _dot(a, b):
    return jnp.dot(a, b, preferred_element_type=F32)


def _dot_nt(a, b):
    return lax.dot_general(a, b, (((1,), (1,)), ((), ())), preferred_element_type=F32)


def _rms(x, g):
    return x * lax.rsqrt(jnp.mean(x * x, axis=-1, keepdims=True) + NORM_EPS) * g


def _sigmoid(x):
    return 1.0 / (1.0 + jnp.exp(-x))


def _ffn_kernel(*refs, n_pre, has_bias, final_norm):
    it = iter(refs)
    x_ref = next(it)
    pre = [(next(it), next(it)) for _ in range(n_pre)]
    bias_ref = next(it) if has_bias else None
    g_ref, wg_ref, wu_ref, wd_ref = next(it), next(it), next(it), next(it)
    gf_ref = next(it) if final_norm else None
    o_ref = next(it)

    x = x_ref[...]
    if n_pre:
        upd = _dot(pre[0][0][...], pre[0][1][...])
        for a_ref, w_ref in pre[1:]:
            upd = upd + _dot(a_ref[...], w_ref[...])
        if has_bias:
            upd = upd + bias_ref[...]
        x = x + upd
    xn = _rms(x, g_ref[...]).astype(BF16)
    gate = _dot(xn, wg_ref[...])
    up = _dot(xn, wu_ref[...])
    mid = (gate * _sigmoid(gate) * up).astype(BF16)
    y = x + FFN_RES_WEIGHT * _dot(mid, wd_ref[...])
    if final_norm:
        y = _rms(y, gf_ref[...])
    o_ref[...] = y


def _const_spec(shape):
    nd = len(shape)
    return pl.BlockSpec(shape, lambda *_: (0,) * nd, pipeline_mode=pl.Buffered(1))


def _ffn(x2d, g, wg, wu, wd, pre=(), bias=None, final_g=None, tm=512):
    n = x2d.shape[0]
    tm = min(tm, n)
    row = lambda w: pl.BlockSpec((tm, w), lambda i: (i, 0))
    args, specs = [x2d], [row(D_MODEL)]
    for act, w in pre:
        args += [act, w]
        specs += [row(act.shape[1]), _const_spec(w.shape)]
    if bias is not None:
        args.append(bias)
        specs.append(_const_spec(bias.shape))
    args += [g, wg, wu, wd]
    specs += [_const_spec(g.shape), _const_spec(wg.shape), _const_spec(wu.shape), _const_spec(wd.shape)]
    if final_g is not None:
        args.append(final_g)
        specs.append(_const_spec(final_g.shape))
    kern = functools.partial(_ffn_kernel, n_pre=len(pre), has_bias=bias is not None,
                             final_norm=final_g is not None)
    return pl.pallas_call(
        kern, grid=(n // tm,), in_specs=specs, out_specs=row(D_MODEL),
        out_shape=jax.ShapeDtypeStruct((n, D_MODEL), F32),
        compiler_params=pltpu.CompilerParams(dimension_semantics=("arbitrary",),
                                             vmem_limit_bytes=VMEM_LIMIT),
        name="ffn",
    )(*args)


def _rope(z, cos, sin, half, period):
    width = z.shape[-1]
    lane = lax.broadcasted_iota(I32, z.shape, 1)
    first = (lane & (period - 1)) < half
    partner = jnp.where(first, pltpu.roll(z, width - half, 1), pltpu.roll(z, half, 1))
    reps = width // LANES
    if reps > 1:
        cos = jnp.concatenate([cos] * reps, axis=1)
        sin = jnp.concatenate([sin] * reps, axis=1)
    return z * cos + partner * sin


def _mix_in_kernel(x_ref, g_ref, wa_ref, wt_ref, wb_ref, cw_ref, tab_ref,
                   q_ref, qi_ref, k_ref, vt_ref, ki_ref, wit_ref, yc_ref,
                   cbuf_ref, carry_ref):
    s = pl.program_id(0)
    b = pl.program_id(1)
    tm = x_ref.shape[1]
    xn = _rms(x_ref[0], g_ref[...]).astype(BF16)
    za = _dot(xn, wa_ref[...])
    cos_a, sin_a = tab_ref[:, 0:LANES], tab_ref[:, LANES:2 * LANES]
    cos_i, sin_i = tab_ref[:, 2 * LANES:3 * LANES], tab_ref[:, 3 * LANES:4 * LANES]
    half_a = ATT_HEAD_DIM // ROT_FRACTION // 2
    half_i = IDX_DIM // ROT_FRACTION // 2

    q = _rope(za[:, A_Q:A_Q + ATT_WIDTH], cos_a, sin_a, half_a, ATT_HEAD_DIM)
    q_ref[0] = (q * (ATT_HEAD_DIM ** -0.5 * LOG2E)).astype(BF16)
    qi = _rope(za[:, A_QI:A_QI + IDX_WIDTH], cos_i, sin_i, half_i, IDX_DIM)
    qi_ref[0] = qi.astype(BF16)
    k = _rope(za[:, A_K:A_K + LANES], cos_a, sin_a, half_a, ATT_HEAD_DIM)
    k_ref[0] = k.astype(BF16)
    ki = _rope(za[:, A_KI:A_KI + LANES], cos_i, sin_i, half_i, IDX_DIM)
    ki_ref[0] = ki[:, :IDX_DIM].astype(BF16)
    zt = _dot_nt(wt_ref[...], xn)
    t_row = lax.broadcasted_iota(I32, zt.shape, 0)
    vt_ref[0, 0] = jnp.where(t_row < ATT_HEAD_DIM, zt,
                             jnp.where(t_row == ATT_HEAD_DIM, 1.0, 0.0)).astype(BF16)
    wit_ref[0] = zt[ATT_HEAD_DIM:, :] * ((IDX_HEADS * IDX_DIM) ** -0.5)

    zb = _dot(xn, wb_ref[...])
    gate_b = zb[:, 0:SC_WIDTH]
    cu = zb[:, SC_WIDTH:2 * SC_WIDTH] * zb[:, 2 * SC_WIDTH:3 * SC_WIDTH]

    @pl.when(s == 0)
    def _():
        cbuf_ref[0:SC_HALO, :] = jnp.zeros((SC_HALO, SC_WIDTH), F32)

    @pl.when(s > 0)
    def _():
        cbuf_ref[0:SC_HALO, :] = carry_ref[b]

    cbuf_ref[SC_HALO:SC_HALO + tm, :] = cu
    carry_ref[b] = cu[tm - SC_HALO:tm, :]
    conv = cu * cw_ref[SC_KERNEL - 1:SC_KERNEL, :]
    for j in range(SC_KERNEL - 1):
        off = SC_HALO - (SC_KERNEL - 1) + j
        conv = conv + cbuf_ref[off:off + tm, :] * cw_ref[j:j + 1, :]
    yc_ref[0] = (gate_b * conv).astype(BF16)


def _mix_in(h, g, wa, wt, wb, conv_w, tab):
    bsz, seq, _ = h.shape
    tm = KEY_CHUNK
    tok = lambda w: pl.BlockSpec((1, tm, w), lambda s, b: (b, s, 0))
    sds = jax.ShapeDtypeStruct
    out_shape = [sds((bsz, seq, ATT_WIDTH), BF16), sds((bsz, seq, IDX_WIDTH), BF16),
                 sds((bsz, seq, LANES), BF16), sds((bsz, seq // tm, VT_ROWS, tm), BF16),
                 sds((bsz, seq, IDX_DIM), BF16), sds((bsz, IDX_HEADS, seq), F32),
                 sds((bsz, seq, SC_WIDTH), BF16)]
    out_specs = [tok(ATT_WIDTH), tok(IDX_WIDTH), tok(LANES),
                 pl.BlockSpec((1, 1, VT_ROWS, tm), lambda s, b: (b, s, 0, 0)),
                 tok(IDX_DIM), pl.BlockSpec((1, IDX_HEADS, tm), lambda s, b: (b, 0, s)),
                 tok(SC_WIDTH)]
    return pl.pallas_call(
        _mix_in_kernel, grid=(seq // tm, bsz),
        in_specs=[tok(D_MODEL), _const_spec(g.shape), _const_spec(wa.shape), _const_spec(wt.shape),
                  _const_spec(wb.shape), _const_spec(conv_w.shape),
                  pl.BlockSpec((tm, 4 * LANES), lambda s, b: (s, 0))],
        out_specs=out_specs, out_shape=out_shape,
        scratch_shapes=[pltpu.VMEM((tm + SC_HALO, SC_WIDTH), F32),
                        pltpu.VMEM((bsz, SC_HALO, SC_WIDTH), F32)],
        compiler_params=pltpu.CompilerParams(dimension_semantics=("arbitrary", "arbitrary"),
                                             vmem_limit_bytes=VMEM_LIMIT),
        name="mix_in",
    )(h, g, wa, wt, wb, conv_w, tab)


def _sortable(x):
    bits = pltpu.bitcast(x + 0.0, I32)
    return bits ^ ((bits >> 31) & 0x7FFFFFFF)


def _count_ge(arr_ref, n_steps, group, cand):
    kc = arr_ref.shape[1]
    c16 = cand.astype(I16)
    one, zero = jnp.bfloat16(1), jnp.bfloat16(0)
    n_acc = 4

    def body(s, accs):
        accs = list(accs)
        for u in range(group):
            for r in range(kc // PACK_ROWS):
                t = arr_ref[s * group + u, r * PACK_ROWS:(r + 1) * PACK_ROWS, :]
                accs[r % n_acc] = accs[r % n_acc] + jnp.where(t >= c16, one, zero)
        return tuple(accs)

    accs = lax.fori_loop(0, n_steps, body,
                         tuple(jnp.zeros((PACK_ROWS, Q_BLOCK), BF16) for _ in range(n_acc)))
    total = (accs[0].astype(F32) + accs[1].astype(F32)) + (accs[2].astype(F32) + accs[3].astype(F32))
    return jnp.sum(total, axis=0, keepdims=True)


def _descend(arr_ref, n_steps, group, base, kf, state, first_bit, n_bits):
    def step(t, carry):
        thr, cge, cgt = carry
        cand = thr | (jnp.int32(1) << (first_bit - t))
        n = base + _count_ge(arr_ref, n_steps, group, cand - HALF_BIAS)
        take = n >= kf
        return jnp.where(take, cand, thr), jnp.where(take, n, cge), jnp.where(take, cgt, n)

    return lax.fori_loop(0, n_bits, step, state)


def _dsa_kernel(q_ref, qi_ref, wit_ref, k_ref, vt_ref, ki_ref, o_ref,
                qs_ref, qis_ref, hi_ref, lo_ref, low_ref, bias_ref, thr_ref, need_ref, acc_ref,
                lga_ref, lgb_ref, *, topk, group):
    i = pl.program_id(1)
    kc = KEY_CHUNK
    rows = ATT_HEADS * Q_BLOCK
    n_chunks = (i * Q_BLOCK + Q_BLOCK + kc - 1) // kc
    n_pairs = (n_chunks + 1) // 2
    q_pos = i * Q_BLOCK + lax.broadcasted_iota(I32, (kc, Q_BLOCK), 1)
    kf = jnp.float32(topk)

    r_i = lax.broadcasted_iota(I32, (Q_BLOCK, Q_BLOCK), 0)
    c_i = lax.broadcasted_iota(I32, (Q_BLOCK, Q_BLOCK), 1)
    eye = jnp.where(r_i == c_i, 1.0, 0.0).astype(BF16)
    for h in range(ATT_HEADS):
        blk_rows = slice(h * Q_BLOCK, (h + 1) * Q_BLOCK)
        qs_ref[blk_rows, 0:LANES] = jnp.zeros((Q_BLOCK, LANES), BF16)
        qs_ref[blk_rows, 0:ATT_HEAD_DIM] = q_ref[0, :, h * ATT_HEAD_DIM:(h + 1) * ATT_HEAD_DIM]
        qs_ref[blk_rows, LANES:QS_COLS] = eye
    for h in range(IDX_HEADS):
        qis_ref[h * Q_BLOCK:(h + 1) * Q_BLOCK, :] = qi_ref[0, :, h * IDX_DIM:(h + 1) * IDX_DIM]

    wit = wit_ref[0]

    def score_chunk(c, diagonal):
        start = pl.multiple_of(c * kc, kc)
        ki_c = ki_ref[0, pl.ds(start, kc), :]
        score = jnp.zeros((kc, Q_BLOCK), F32)
        for g in range(IDX_HEADS // GROUP_HEADS):
            dots = _dot_nt(ki_c, qis_ref[g * GROUP_ROWS:(g + 1) * GROUP_ROWS, :])
            for j in range(GROUP_HEADS):
                h = g * GROUP_HEADS + j
                score = score + wit[h:h + 1, :] * jnp.maximum(dots[:, j * Q_BLOCK:(j + 1) * Q_BLOCK], 0.0)
        if diagonal:
            k_pos = start + lax.broadcasted_iota(I32, (kc, Q_BLOCK), 0)
            score = jnp.where(k_pos <= q_pos, score, -jnp.inf)
        key = _sortable(score)
        hi_ref[c] = (key >> 16).astype(I16)
        lo_ref[c] = ((key & 0xFFFF) - HALF_BIAS).astype(I16)

    def score_pair(p, carry):
        score_chunk(2 * p, False)
        score_chunk(2 * p + 1, False)
        return carry

    lax.fori_loop(0, n_pairs - 1, score_pair, 0)
    score_chunk(2 * n_pairs - 2, True)

    @pl.when(n_chunks % 2 == 0)
    def _():
        score_chunk(2 * n_pairs - 1, True)

    n_steps = (n_chunks + group - 1) // group

    def pad_chunk(c, carry):
        hi_ref[c] = jnp.full((kc, Q_BLOCK), NEG_INF_KEY >> 16, I16)
        lo_ref[c] = jnp.full((kc, Q_BLOCK), (NEG_INF_KEY & 0xFFFF) - HALF_BIAS, I16)
        return carry

    lax.fori_loop(n_chunks, n_steps * group, pad_chunk, 0)

    @pl.when((i + 1) * Q_BLOCK <= topk)
    def _():
        above = NEG_INF_KEY + 1
        thr_ref[0] = jnp.full((1, Q_BLOCK), above >> 16, I32)
        thr_ref[1] = jnp.full((1, Q_BLOCK), (above & 0xFFFF) - HALF_BIAS, I32)
        need_ref[0] = jnp.zeros((1, Q_BLOCK), F32)
        need_ref[1] = jnp.zeros((1, Q_BLOCK), F32)

    @pl.when((i + 1) * Q_BLOCK > topk)
    def _():
        total = jnp.full((1, Q_BLOCK), n_steps * group * kc, I32).astype(F32)
        zero = jnp.zeros((1, Q_BLOCK), F32)
        zero_i = jnp.zeros((1, Q_BLOCK), I32)

        t_hi, cge_hi, cgt_hi = _descend(hi_ref, n_steps, group, zero, kf, (zero_i, total, zero), 15, 16)
        t_hi = t_hi - HALF_BIAS
        t_hi16 = t_hi.astype(I16)

        def bucket_step(s, carry):
            for u in range(group):
                c = s * group + u
                low_ref[c] = jnp.where(hi_ref[c] == t_hi16, lo_ref[c], jnp.int16(-HALF_BIAS))
            return carry

        lax.fori_loop(0, n_steps, bucket_step, 0)

        t_lo, cge, cgt = _descend(low_ref, n_steps, group, cgt_hi, kf, (zero_i, cge_hi, cgt_hi),
                                  15, LO_FIRST_BITS)
        thr_ref[1] = t_lo
        need_ref[0] = cge
        need_ref[1] = cgt

        @pl.when(jnp.max(jnp.abs(cge - kf)) > 0.0)
        def _():
            t, c_ge, c_gt = _descend(low_ref, n_steps, group, cgt_hi, kf,
                                     (thr_ref[1], need_ref[0], need_ref[1]),
                                     15 - LO_FIRST_BITS, 16 - LO_FIRST_BITS)
            thr_ref[1] = t
            need_ref[0] = c_ge
            need_ref[1] = c_gt

        cge, cgt = need_ref[0], need_ref[1]
        thr_ref[0] = t_hi
        thr_ref[1] = thr_ref[1] - HALF_BIAS
        need_ref[0] = kf - cgt
        need_ref[1] = cge - cgt

    t_hi16 = thr_ref[0].astype(I16)
    t_lo16 = thr_ref[1].astype(I16)

    def bias_chunk(c, carry):
        hi, lo = hi_ref[c], lo_ref[c]
        sel = (hi > t_hi16) | ((hi == t_hi16) & (lo >= t_lo16))
        bias_ref[c] = jnp.where(sel, jnp.bfloat16(0.0), jnp.bfloat16(NEG_BIG))
        return carry

    lax.fori_loop(0, n_chunks, bias_chunk, 0)

    @pl.when(jnp.max(need_ref[1] - need_ref[0]) > 0.0)
    def _():
        need = need_ref[0]
        r_i = lax.broadcasted_iota(I32, (kc, kc), 0)
        c_i = lax.broadcasted_iota(I32, (kc, kc), 1)
        tri = jnp.where(c_i <= r_i, 1.0, 0.0).astype(BF16)

        def tie_chunk(c, seen):
            eq = (hi_ref[c] == t_hi16) & (lo_ref[c] == t_lo16)
            eq_b = jnp.where(eq, jnp.bfloat16(1.0), jnp.bfloat16(0.0))
            rank = seen + _dot(tri, eq_b)
            drop = (eq_b.astype(F32) > 0.5) & (rank > need)
            bias_ref[c] = jnp.where(drop, NEG_BIG, bias_ref[c].astype(F32)).astype(BF16)
            return rank[kc - 1:kc, :]

        lax.fori_loop(0, n_chunks, tie_chunk, jnp.zeros((1, Q_BLOCK), F32))

    acc_ref[...] = jnp.zeros(acc_ref.shape, F32)

    n_groups = ATT_HEADS // GROUP_HEADS
    group = lambda g: slice(g * GROUP_ROWS, (g + 1) * GROUP_ROWS)

    def keys_of(c):
        start = pl.multiple_of(c * kc, kc)
        return jnp.concatenate([k_ref[0, pl.ds(start, kc), :], bias_ref[c]], axis=1)

    def softmax_step(lg_ref, g, v_c, m):
        lg = lg_ref[:, group(g)]
        m_new = jnp.maximum(m, jnp.max(lg, axis=0, keepdims=True))
        p = jnp.exp2(lg - m_new)
        acc_ref[:, group(g)] = jnp.exp2(m - m_new) * acc_ref[:, group(g)] + _dot(v_c, p.astype(BF16))
        return m_new

    last = n_chunks - 1
    lga_ref[...] = _dot_nt(keys_of(0), qs_ref[...])

    def half_step(src_ref, dst_ref, c_src, c_dst, ms):
        keys, v_c = keys_of(c_dst), vt_ref[0, c_src]
        out = []
        for g in range(n_groups):
            dst_ref[:, group(g)] = _dot_nt(keys, qs_ref[group(g), :])
            out.append(softmax_step(src_ref, g, v_c, ms[g]))
        return tuple(out)

    def attn_pair(p, ms):
        ms = half_step(lga_ref, lgb_ref, 2 * p, 2 * p + 1, ms)
        return half_step(lgb_ref, lga_ref, 2 * p + 1, jnp.minimum(2 * p + 2, last), ms)

    ms = lax.fori_loop(0, n_chunks // 2, attn_pair,
                       tuple(jnp.full((1, GROUP_ROWS), NEG_BIG, F32) for _ in range(n_groups)))

    @pl.when(n_chunks % 2 == 1)
    def _():
        v_c = vt_ref[0, last]
        for g in range(n_groups):
            softmax_step(lga_ref, g, v_c, ms[g])

    denom = acc_ref[ATT_HEAD_DIM:ATT_HEAD_DIM + 1, :]
    out = (acc_ref[0:ATT_HEAD_DIM, :] / denom).astype(BF16)
    for h in range(ATT_HEADS):
        o_ref[0, :, h * ATT_HEAD_DIM:(h + 1) * ATT_HEAD_DIM] = _dot_nt(
            eye, out[:, h * Q_BLOCK:(h + 1) * Q_BLOCK]).astype(BF16)


def _dsa(q, qi, wi_t, k, v_t, ki):
    bsz, seq, _ = q.shape
    topk = min(TOPK_MAX, seq // 4)
    assert topk % Q_BLOCK == 0 and seq % (2 * KEY_CHUNK) == 0
    n_blk = seq // Q_BLOCK
    n_ck = seq // KEY_CHUNK
    group = SEARCH_GROUP if n_ck % SEARCH_GROUP == 0 else 2
    blk = lambda w: pl.BlockSpec((1, Q_BLOCK, w), lambda b, i: (b, i, 0))
    full = lambda w: pl.BlockSpec((1, seq, w), lambda b, i: (b, 0, 0))
    rows = ATT_HEADS * Q_BLOCK
    return pl.pallas_call(
        functools.partial(_dsa_kernel, topk=topk, group=group), grid=(bsz, n_blk),
        in_specs=[blk(ATT_WIDTH), blk(IDX_WIDTH),
                  pl.BlockSpec((1, IDX_HEADS, Q_BLOCK), lambda b, i: (b, 0, i)),
                  full(LANES),
                  pl.BlockSpec((1, n_ck, VT_ROWS, KEY_CHUNK), lambda b, i: (b, 0, 0, 0)),
                  full(IDX_DIM)],
        out_specs=blk(ATT_WIDTH),
        out_shape=jax.ShapeDtypeStruct((bsz, seq, ATT_WIDTH), BF16),
        scratch_shapes=[pltpu.VMEM((rows, QS_COLS), BF16),
                        pltpu.VMEM((IDX_HEADS * Q_BLOCK, IDX_DIM), BF16),
                        pltpu.VMEM((n_ck, KEY_CHUNK, Q_BLOCK), I16),
                        pltpu.VMEM((n_ck, KEY_CHUNK, Q_BLOCK), I16),
                        pltpu.VMEM((n_ck, KEY_CHUNK, Q_BLOCK), I16),
                        pltpu.VMEM((n_ck, KEY_CHUNK, Q_BLOCK), BF16),
                        pltpu.VMEM((2, 1, Q_BLOCK), I32),
                        pltpu.VMEM((2, 1, Q_BLOCK), F32),
                        pltpu.VMEM((VT_ROWS, rows), F32),
                        pltpu.VMEM((KEY_CHUNK, rows), F32),
                        pltpu.VMEM((KEY_CHUNK, rows), F32)],
        compiler_params=pltpu.CompilerParams(dimension_semantics=("arbitrary", "arbitrary"),
                                             vmem_limit_bytes=VMEM_LIMIT),
        name="dsa",
    )(q, qi, wi_t, k, v_t, ki)


def _conf_in_kernel(x_ref, g_ref, w1_ref, b1_ref, cw_ref, cb_ref, lg_ref, lb_ref, o_ref, ubuf_ref, sh_ref):
    s = pl.program_id(1)
    tm = x_ref.shape[1]
    xn = _rms(x_ref[0], g_ref[...]).astype(BF16)
    z = _dot(xn, w1_ref[...]) + b1_ref[...]
    u = z[:, :CONF_WIDTH] * _sigmoid(z[:, CONF_WIDTH:])

    @pl.when(s == 0)
    def _():
        ubuf_ref[0:CONF_HALO, :] = jnp.zeros((CONF_HALO, CONF_WIDTH), F32)

    @pl.when(s > 0)
    def _():
        ubuf_ref[0:CONF_HALO, :] = ubuf_ref[tm:tm + CONF_HALO, :]

    ubuf_ref[CONF_HALO:CONF_HALO + tm, :] = u
    span = tm + CONF_HALO - SUBLANES
    for r in range(1, SUBLANES):
        sh_ref[r - 1, 0:span, :] = ubuf_ref[r:r + span, :]
    conv = u * cw_ref[CONF_KERNEL - 1:CONF_KERNEL, :] + cb_ref[...]
    for j in range(CONF_KERNEL - 1):
        base, r = divmod(CONF_HALO - (CONF_KERNEL - 1) + j, SUBLANES)
        base *= SUBLANES
        win = ubuf_ref[base:base + tm, :] if r == 0 else sh_ref[r - 1, base:base + tm, :]
        conv = conv + win * cw_ref[j:j + 1, :]
    mu = jnp.mean(conv, axis=-1, keepdims=True)
    xc = conv - mu
    var = jnp.mean(xc * xc, axis=-1, keepdims=True)
    y = xc * lax.rsqrt(var + NORM_EPS) * lg_ref[...] + lb_ref[...]
    o_ref[0] = (y * _sigmoid(y)).astype(BF16)


def _conf_in(h, g, w1, b1, conv_w, conv_b, ln_g, ln_b, tm=256):
    bsz, seq, _ = h.shape
    tm = min(tm, seq)
    tok = lambda w: pl.BlockSpec((1, tm, w), lambda b, s: (b, s, 0))
    consts = (g, w1, b1, conv_w, conv_b, ln_g, ln_b)
    return pl.pallas_call(
        _conf_in_kernel, grid=(bsz, seq // tm),
        in_specs=[tok(D_MODEL)] + [_const_spec(c.shape) for c in consts],
        out_specs=tok(CONF_WIDTH),
        out_shape=jax.ShapeDtypeStruct((bsz, seq, CONF_WIDTH), BF16),
        scratch_shapes=[pltpu.VMEM((tm + CONF_HALO, CONF_WIDTH), F32),
                        pltpu.VMEM((SUBLANES - 1, tm + CONF_HALO - SUBLANES, CONF_WIDTH), F32)],
        compiler_params=pltpu.CompilerParams(dimension_semantics=("arbitrary", "arbitrary"),
                                             vmem_limit_bytes=VMEM_LIMIT),
        name="conf_in",
    )(h, *consts)


def _rope_table(seq, head_dim):
    half = head_dim // ROT_FRACTION // 2
    inv_freq = ROPE_THETA ** (-jnp.arange(half, dtype=F32) / half)
    ang = jnp.arange(seq, dtype=F32)[:, None] * inv_freq[None, :]
    cos, sin = jnp.cos(ang), jnp.sin(ang)
    rest = head_dim - 2 * half
    cos_h = jnp.concatenate([cos, cos, jnp.ones((seq, rest), F32)], axis=1)
    sin_h = jnp.concatenate([-sin, sin, jnp.zeros((seq, rest), F32)], axis=1)
    reps = LANES // head_dim
    return jnp.tile(cos_h, (1, reps)), jnp.tile(sin_h, (1, reps))


def _pack_w_in(w_in):
    sizes = (ATT_WIDTH, ATT_HEAD_DIM, ATT_HEAD_DIM, IDX_WIDTH, IDX_DIM, IDX_HEADS)
    offs = [0]
    for sz in sizes:
        offs.append(offs[-1] + sz)
    q, k, v, qi, ki, wi = (w_in[:, offs[j]:offs[j + 1]] for j in range(len(sizes)))
    pad = lambda w: jnp.pad(w, ((0, 0), (0, LANES - w.shape[1])))
    wa = jnp.concatenate([q, qi, pad(k), pad(ki)], axis=1)
    wt = jnp.concatenate([v, wi], axis=1).T
    return wa.astype(BF16), wt.astype(BF16), w_in[:, offs[-1]:].astype(BF16)


def kernel(x, ffn_norm, ffn_w_gate, ffn_w_up, ffn_w_down, mix_norm, hyb_w_in, hyb_conv_w, hyb_w_out,
           conf_w_pw1, conf_b_pw1, conf_conv_w, conf_conv_b, conf_ln_g, conf_ln_b, conf_w_pw2,
           conf_b_pw2, final_norm):
    bsz, seq, _ = x.shape
    n = bsz * seq
    depth = ffn_norm.shape[0]
    row = lambda a: a.reshape(1, -1)
    ffn_w = lambda layer, j: (row(ffn_norm[layer, j]), ffn_w_gate[layer, j].astype(BF16),
                              ffn_w_up[layer, j].astype(BF16), ffn_w_down[layer, j].astype(BF16))
    cos_a, sin_a = _rope_table(seq, ATT_HEAD_DIM)
    cos_i, sin_i = _rope_table(seq, IDX_DIM)
    tab = jnp.concatenate([cos_a, sin_a, cos_i, sin_i], axis=1)

    h = x.reshape(n, D_MODEL)
    for layer in range(depth):
        last = layer == depth - 1
        h = _ffn(h, *ffn_w(layer, 0))
        h3 = h.reshape(bsz, seq, D_MODEL)
        if layer % 2 == 0:
            e = layer // 2
            wa, wt, wb = _pack_w_in(hyb_w_in[e])
            q, qi, k, v_t, ki, wi_t, y_conv = _mix_in(h3, row(mix_norm[layer]), wa, wt, wb,
                                                      hyb_conv_w[e], tab)
            y_attn = _dsa(q, qi, wi_t, k, v_t, ki)
            w_out = hyb_w_out[e].astype(BF16)
            pre = ((y_attn.reshape(n, ATT_WIDTH), w_out[:ATT_WIDTH]),
                   (y_conv.reshape(n, SC_WIDTH), w_out[ATT_WIDTH:]))
            bias = None
        else:
            o = layer // 2
            u = _conf_in(h3, row(mix_norm[layer]), conf_w_pw1[o].astype(BF16), row(conf_b_pw1[o]),
                         conf_conv_w[o], row(conf_conv_b[o]), row(conf_ln_g[o]), row(conf_ln_b[o]))
            pre = ((u.reshape(n, CONF_WIDTH), conf_w_pw2[o].astype(BF16)),)
            bias = row(conf_b_pw2[o])
        h = _ffn(h, *ffn_w(layer, 1), pre=pre, bias=bias,
                 final_g=row(final_norm) if last else None)
    return h.reshape(bsz, seq, D_MODEL)
```

```python
import functools

import jax
import jax.numpy as jnp
from jax import lax
from jax.experimental import pallas as pl
from jax.experimental.pallas import tpu as pltpu

F32 = jnp.float32
BF16 = jnp.bfloat16
I32 = jnp.int32
I16 = jnp.int16

D_MODEL = 1024
D_FF = 2816
FFN_RES_WEIGHT = 0.5
ATT_HEADS = 8
ATT_HEAD_DIM = 64
ATT_WIDTH = ATT_HEADS * ATT_HEAD_DIM
IDX_HEADS = 8
IDX_DIM = 32
IDX_WIDTH = IDX_HEADS * IDX_DIM
TOPK_MAX = 256
Q_BLOCK = 128
SC_WIDTH = D_MODEL - ATT_WIDTH
SC_KERNEL = 3
CONF_WIDTH = D_MODEL
CONF_KERNEL = 31
ROPE_THETA = 500000.0
ROT_FRACTION = 4
NORM_EPS = 1e-6

LANES = 128
SUBLANES = 8
PACK_ROWS = 16
VMEM_LIMIT = 56 * 1024 * 1024

A_Q = 0
A_QI = A_Q + ATT_WIDTH
A_K = A_QI + IDX_WIDTH
A_KI = A_K + LANES
A_COLS = A_KI + LANES

KEY_CHUNK = 512
VT_ROWS = ATT_HEAD_DIM + IDX_HEADS
QS_COLS = 2 * LANES
COUNT_ACCS = 4
BF16_EXACT_INT = 256
LO_FIRST_BITS = 11
LOG2E = 1.4426950408889634
GROUP_HEADS = 2
GROUP_ROWS = GROUP_HEADS * Q_BLOCK
NEG_BIG = -1e30
HALF_BIAS = 1 << 15
NEG_INF_KEY = -2 ** 31 + 0x007FFFFF
CONF_HALO = 32
SC_HALO = 8


def _dot(a, b):
    return jnp.dot(a, b, preferred_element_type=F32)


def _dot_nt(a, b):
    return lax.dot_general(a, b, (((1,), (1,)), ((), ())), preferred_element_type=F32)


def _rms(x, g):
    return x * lax.rsqrt(jnp.mean(x * x, axis=-1, keepdims=True) + NORM_EPS) * g


def _sigmoid(x):
    return 1.0 / (1.0 + jnp.exp(-x))


def _ffn_kernel(*refs, n_pre, has_bias, final_norm):
    it = iter(refs)
    x_ref = next(it)
    pre = [(next(it), next(it)) for _ in range(n_pre)]
    bias_ref = next(it) if has_bias else None
    g_ref, wg_ref, wu_ref, wd_ref = next(it), next(it), next(it), next(it)
    gf_ref = next(it) if final_norm else None
    o_ref = next(it)

    x = x_ref[...]
    if n_pre:
        upd = _dot(pre[0][0][...], pre[0][1][...])
        for a_ref, w_ref in pre[1:]:
            upd = upd + _dot(a_ref[...], w_ref[...])
        if has_bias:
            upd = upd + bias_ref[...]
        x = x + upd
    xn = _rms(x, g_ref[...]).astype(BF16)
    gate = _dot(xn, wg_ref[...])
    up = _dot(xn, wu_ref[...])
    mid = (gate * _sigmoid(gate) * up).astype(BF16)
    y = x + FFN_RES_WEIGHT * _dot(mid, wd_ref[...])
    if final_norm:
        y = _rms(y, gf_ref[...])
    o_ref[...] = y


def _const_spec(shape):
    nd = len(shape)
    return pl.BlockSpec(shape, lambda *_: (0,) * nd, pipeline_mode=pl.Buffered(1))


def _ffn(x2d, g, wg, wu, wd, pre=(), bias=None, final_g=None, tm=512):
    n = x2d.shape[0]
    tm = min(tm, n)
    row = lambda w: pl.BlockSpec((tm, w), lambda i: (i, 0))
    args, specs = [x2d], [row(D_MODEL)]
    for act, w in pre:
        args += [act, w]
        specs += [row(act.shape[1]), _const_spec(w.shape)]
    if bias is not None:
        args.append(bias)
        specs.append(_const_spec(bias.shape))
    args += [g, wg, wu, wd]
    specs += [_const_spec(g.shape), _const_spec(wg.shape), _const_spec(wu.shape), _const_spec(wd.shape)]
    if final_g is not None:
        args.append(final_g)
        specs.append(_const_spec(final_g.shape))
    kern = functools.partial(_ffn_kernel, n_pre=len(pre), has_bias=bias is not None,
                             final_norm=final_g is not None)
    return pl.pallas_call(
        kern, grid=(n // tm,), in_specs=specs, out_specs=row(D_MODEL),
        out_shape=jax.ShapeDtypeStruct((n, D_MODEL), F32),
        compiler_params=pltpu.CompilerParams(dimension_semantics=("arbitrary",),
                                             vmem_limit_bytes=VMEM_LIMIT),
        name="ffn",
    )(*args)


def _rope(z, cos, sin, half, period):
    width = z.shape[-1]
    lane = lax.broadcasted_iota(I32, z.shape, 1)
    first = (lane & (period - 1)) < half
    partner = jnp.where(first, pltpu.roll(z, width - half, 1), pltpu.roll(z, half, 1))
    reps = width // LANES
    if reps > 1:
        cos = jnp.concatenate([cos] * reps, axis=1)
        sin = jnp.concatenate([sin] * reps, axis=1)
    return z * cos + partner * sin


def _mix_in_kernel(x_ref, g_ref, wa_ref, wt_ref, wb_ref, cw_ref, tab_ref,
                   q_ref, qi_ref, k_ref, vt_ref, ki_ref, wit_ref, yc_ref,
                   cbuf_ref, carry_ref):
    s = pl.program_id(0)
    b = pl.program_id(1)
    tm = x_ref.shape[1]
    xn = _rms(x_ref[0], g_ref[...]).astype(BF16)
    za = _dot(xn, wa_ref[...])
    cos_a, sin_a = tab_ref[:, 0:LANES], tab_ref[:, LANES:2 * LANES]
    cos_i, sin_i = tab_ref[:, 2 * LANES:3 * LANES], tab_ref[:, 3 * LANES:4 * LANES]
    half_a = ATT_HEAD_DIM // ROT_FRACTION // 2
    half_i = IDX_DIM // ROT_FRACTION // 2

    q = _rope(za[:, A_Q:A_Q + ATT_WIDTH], cos_a, sin_a, half_a, ATT_HEAD_DIM)
    q_ref[0] = (q * (ATT_HEAD_DIM ** -0.5 * LOG2E)).astype(BF16)
    qi = _rope(za[:, A_QI:A_QI + IDX_WIDTH], cos_i, sin_i, half_i, IDX_DIM)
    qi_ref[0] = qi.astype(BF16)
    k = _rope(za[:, A_K:A_K + LANES], cos_a, sin_a, half_a, ATT_HEAD_DIM)
    k_ref[0] = k.astype(BF16)
    ki = _rope(za[:, A_KI:A_KI + LANES], cos_i, sin_i, half_i, IDX_DIM)
    ki_ref[0] = ki[:, :IDX_DIM].astype(BF16)
    zt = _dot_nt(wt_ref[...], xn)
    t_row = lax.broadcasted_iota(I32, zt.shape, 0)
    vt_ref[0, 0] = jnp.where(t_row < ATT_HEAD_DIM, zt,
                             jnp.where(t_row == ATT_HEAD_DIM, 1.0, 0.0)).astype(BF16)
    wit_ref[0] = zt[ATT_HEAD_DIM:, :] * ((IDX_HEADS * IDX_DIM) ** -0.5)

    zb = _dot(xn, wb_ref[...])
    gate_b = zb[:, 0:SC_WIDTH]
    cu = zb[:, SC_WIDTH:2 * SC_WIDTH] * zb[:, 2 * SC_WIDTH:3 * SC_WIDTH]

    @pl.when(s == 0)
    def _():
        cbuf_ref[0:SC_HALO, :] = jnp.zeros((SC_HALO, SC_WIDTH), F32)

    @pl.when(s > 0)
    def _():
        cbuf_ref[0:SC_HALO, :] = carry_ref[b]

    cbuf_ref[SC_HALO:SC_HALO + tm, :] = cu
    carry_ref[b] = cu[tm - SC_HALO:tm, :]
    conv = cu * cw_ref[SC_KERNEL - 1:SC_KERNEL, :]
    for j in range(SC_KERNEL - 1):
        off = SC_HALO - (SC_KERNEL - 1) + j
        conv = conv + cbuf_ref[off:off + tm, :] * cw_ref[j:j + 1, :]
    yc_ref[0] = (gate_b * conv).astype(BF16)


def _mix_in(h, g, wa, wt, wb, conv_w, tab):
    bsz, seq, _ = h.shape
    tm = KEY_CHUNK
    tok = lambda w: pl.BlockSpec((1, tm, w), lambda s, b: (b, s, 0))
    sds = jax.ShapeDtypeStruct
    out_shape = [sds((bsz, seq, ATT_WIDTH), BF16), sds((bsz, seq, IDX_WIDTH), BF16),
                 sds((bsz, seq, LANES), BF16), sds((bsz, seq // tm, VT_ROWS, tm), BF16),
                 sds((bsz, seq, IDX_DIM), BF16), sds((bsz, IDX_HEADS, seq), F32),
                 sds((bsz, seq, SC_WIDTH), BF16)]
    out_specs = [tok(ATT_WIDTH), tok(IDX_WIDTH), tok(LANES),
                 pl.BlockSpec((1, 1, VT_ROWS, tm), lambda s, b: (b, s, 0, 0)),
                 tok(IDX_DIM), pl.BlockSpec((1, IDX_HEADS, tm), lambda s, b: (b, 0, s)),
                 tok(SC_WIDTH)]
    return pl.pallas_call(
        _mix_in_kernel, grid=(seq // tm, bsz),
        in_specs=[tok(D_MODEL), _const_spec(g.shape), _const_spec(wa.shape), _const_spec(wt.shape),
                  _const_spec(wb.shape), _const_spec(conv_w.shape),
                  pl.BlockSpec((tm, 4 * LANES), lambda s, b: (s, 0))],
        out_specs=out_specs, out_shape=out_shape,
        scratch_shapes=[pltpu.VMEM((tm + SC_HALO, SC_WIDTH), F32),
                        pltpu.VMEM((bsz, SC_HALO, SC_WIDTH), F32)],
        compiler_params=pltpu.CompilerParams(dimension_semantics=("arbitrary", "arbitrary"),
                                             vmem_limit_bytes=VMEM_LIMIT),
        name="mix_in",
    )(h, g, wa, wt, wb, conv_w, tab)


def _sortable(x):
    bits = pltpu.bitcast(x + 0.0, I32)
    return bits ^ ((bits >> 31) & 0x7FFFFFFF)


def _count_ge(arr_ref, n_chunks, cand):
    kc = arr_ref.shape[1]
    c16 = cand.astype(I16)
    one, zero = jnp.bfloat16(1), jnp.bfloat16(0)

    def body(c, accs):
        accs = list(accs)
        for r in range(kc // PACK_ROWS):
            t = arr_ref[c, r * PACK_ROWS:(r + 1) * PACK_ROWS, :]
            accs[r % COUNT_ACCS] = accs[r % COUNT_ACCS] + jnp.where(t >= c16, one, zero)
        return tuple(accs)

    accs = lax.fori_loop(0, n_chunks, body,
                         tuple(jnp.zeros((PACK_ROWS, Q_BLOCK), BF16) for _ in range(COUNT_ACCS)))
    total = accs[0].astype(F32)
    for acc in accs[1:]:
        total = total + acc.astype(F32)
    return jnp.sum(total, axis=0, keepdims=True)


def _descend(arr_ref, n_chunks, base, kf, state, first_bit, n_bits):
    def step(t, carry):
        thr, cge, cgt = carry
        cand = thr | (jnp.int32(1) << (first_bit - t))
        n = base + _count_ge(arr_ref, n_chunks, cand - HALF_BIAS)
        take = n >= kf
        return jnp.where(take, cand, thr), jnp.where(take, n, cge), jnp.where(take, cgt, n)

    return lax.fori_loop(0, n_bits, step, state)


def _dsa_kernel(q_ref, qi_ref, wit_ref, k_ref, vt_ref, ki_ref, o_ref,
                qs_ref, qis_ref, hi_ref, lo_ref, low_ref, bias_ref, thr_ref, need_ref, acc_ref,
                lga_ref, lgb_ref, *, topk):
    i = pl.program_id(1)
    kc = KEY_CHUNK
    rows = ATT_HEADS * Q_BLOCK
    n_chunks = (i * Q_BLOCK + Q_BLOCK + kc - 1) // kc
    n_pairs = (n_chunks + 1) // 2
    q_pos = i * Q_BLOCK + lax.broadcasted_iota(I32, (kc, Q_BLOCK), 1)
    kf = jnp.float32(topk)

    r_i = lax.broadcasted_iota(I32, (Q_BLOCK, Q_BLOCK), 0)
    c_i = lax.broadcasted_iota(I32, (Q_BLOCK, Q_BLOCK), 1)
    eye = jnp.where(r_i == c_i, 1.0, 0.0).astype(BF16)
    for h in range(ATT_HEADS):
        blk_rows = slice(h * Q_BLOCK, (h + 1) * Q_BLOCK)
        qs_ref[blk_rows, 0:LANES] = jnp.zeros((Q_BLOCK, LANES), BF16)
        qs_ref[blk_rows, 0:ATT_HEAD_DIM] = q_ref[0, :, h * ATT_HEAD_DIM:(h + 1) * ATT_HEAD_DIM]
        qs_ref[blk_rows, LANES:QS_COLS] = eye
    for h in range(IDX_HEADS):
        qis_ref[h * Q_BLOCK:(h + 1) * Q_BLOCK, :] = qi_ref[0, :, h * IDX_DIM:(h + 1) * IDX_DIM]

    wit = wit_ref[0]

    def score_chunk(c, diagonal):
        start = pl.multiple_of(c * kc, kc)
        ki_c = ki_ref[0, pl.ds(start, kc), :]
        score = jnp.zeros((kc, Q_BLOCK), F32)
        for g in range(IDX_HEADS // GROUP_HEADS):
            dots = _dot_nt(ki_c, qis_ref[g * GROUP_ROWS:(g + 1) * GROUP_ROWS, :])
            for j in range(GROUP_HEADS):
                h = g * GROUP_HEADS + j
                score = score + wit[h:h + 1, :] * jnp.maximum(dots[:, j * Q_BLOCK:(j + 1) * Q_BLOCK], 0.0)
        if diagonal:
            k_pos = start + lax.broadcasted_iota(I32, (kc, Q_BLOCK), 0)
            score = jnp.where(k_pos <= q_pos, score, -jnp.inf)
        key = _sortable(score)
        hi_ref[c] = (key >> 16).astype(I16)
        lo_ref[c] = ((key & 0xFFFF) - HALF_BIAS).astype(I16)

    def score_pair(p, carry):
        score_chunk(2 * p, False)
        score_chunk(2 * p + 1, False)
        return carry

    lax.fori_loop(0, n_pairs - 1, score_pair, 0)
    score_chunk(2 * n_pairs - 2, True)

    @pl.when(n_chunks % 2 == 0)
    def _():
        score_chunk(2 * n_pairs - 1, True)

    @pl.when((i + 1) * Q_BLOCK <= topk)
    def _():
        above = NEG_INF_KEY + 1
        thr_ref[0] = jnp.full((1, Q_BLOCK), above >> 16, I32)
        thr_ref[1] = jnp.full((1, Q_BLOCK), (above & 0xFFFF) - HALF_BIAS, I32)
        need_ref[0] = jnp.zeros((1, Q_BLOCK), F32)
        need_ref[1] = jnp.zeros((1, Q_BLOCK), F32)

    @pl.when((i + 1) * Q_BLOCK > topk)
    def _():
        total = jnp.full((1, Q_BLOCK), n_chunks * kc, I32).astype(F32)
        zero = jnp.zeros((1, Q_BLOCK), F32)
        zero_i = jnp.zeros((1, Q_BLOCK), I32)

        t_hi, cge_hi, cgt_hi = _descend(hi_ref, n_chunks, zero, kf, (zero_i, total, zero), 15, 16)
        t_hi = t_hi - HALF_BIAS
        t_hi16 = t_hi.astype(I16)

        def bucket_chunk(c, carry):
            low_ref[c] = jnp.where(hi_ref[c] == t_hi16, lo_ref[c], jnp.int16(-HALF_BIAS))
            return carry

        lax.fori_loop(0, n_chunks, bucket_chunk, 0)

        t_lo, cge, cgt = _descend(low_ref, n_chunks, cgt_hi, kf, (zero_i, cge_hi, cgt_hi),
                                  15, LO_FIRST_BITS)
        thr_ref[1] = t_lo
        need_ref[0] = cge
        need_ref[1] = cgt

        @pl.when(jnp.max(jnp.abs(cge - kf)) > 0.0)
        def _():
            t, c_ge, c_gt = _descend(low_ref, n_chunks, cgt_hi, kf,
                                     (thr_ref[1], need_ref[0], need_ref[1]),
                                     15 - LO_FIRST_BITS, 16 - LO_FIRST_BITS)
            thr_ref[1] = t
            need_ref[0] = c_ge
            need_ref[1] = c_gt

        cge, cgt = need_ref[0], need_ref[1]
        thr_ref[0] = t_hi
        thr_ref[1] = thr_ref[1] - HALF_BIAS
        need_ref[0] = kf - cgt
        need_ref[1] = cge - cgt

    t_hi16 = thr_ref[0].astype(I16)
    t_lo16 = thr_ref[1].astype(I16)

    def bias_chunk(c, carry):
        hi, lo = hi_ref[c], lo_ref[c]
        sel = (hi > t_hi16) | ((hi == t_hi16) & (lo >= t_lo16))
        bias_ref[c] = jnp.where(sel, jnp.bfloat16(0.0), jnp.bfloat16(NEG_BIG))
        return carry

    lax.fori_loop(0, n_chunks, bias_chunk, 0)

    @pl.when(jnp.max(need_ref[1] - need_ref[0]) > 0.0)
    def _():
        need = need_ref[0]
        r_i = lax.broadcasted_iota(I32, (kc, kc), 0)
        c_i = lax.broadcasted_iota(I32, (kc, kc), 1)
        tri = jnp.where(c_i <= r_i, 1.0, 0.0).astype(BF16)

        def tie_chunk(c, seen):
            eq = (hi_ref[c] == t_hi16) & (lo_ref[c] == t_lo16)
            eq_b = jnp.where(eq, jnp.bfloat16(1.0), jnp.bfloat16(0.0))
            rank = seen + _dot(tri, eq_b)
            drop = (eq_b.astype(F32) > 0.5) & (rank > need)
            bias_ref[c] = jnp.where(drop, NEG_BIG, bias_ref[c].astype(F32)).astype(BF16)
            return rank[kc - 1:kc, :]

        lax.fori_loop(0, n_chunks, tie_chunk, jnp.zeros((1, Q_BLOCK), F32))

    acc_ref[...] = jnp.zeros(acc_ref.shape, F32)

    n_groups = ATT_HEADS // GROUP_HEADS
    group = lambda g: slice(g * GROUP_ROWS, (g + 1) * GROUP_ROWS)

    def keys_of(c):
        start = pl.multiple_of(c * kc, kc)
        return jnp.concatenate([k_ref[0, pl.ds(start, kc), :], bias_ref[c]], axis=1)

    def softmax_step(lg_ref, g, v_c, m):
        lg = lg_ref[:, group(g)]
        m_new = jnp.maximum(m, jnp.max(lg, axis=0, keepdims=True))
        p = jnp.exp2(lg - m_new)
        acc_ref[:, group(g)] = jnp.exp2(m - m_new) * acc_ref[:, group(g)] + _dot(v_c, p.astype(BF16))
        return m_new

    last = n_chunks - 1
    lga_ref[...] = _dot_nt(keys_of(0), qs_ref[...])

    def half_step(src_ref, dst_ref, c_src, c_dst, ms):
        keys, v_c = keys_of(c_dst), vt_ref[0, c_src]
        out = []
        for g in range(n_groups):
            dst_ref[:, group(g)] = _dot_nt(keys, qs_ref[group(g), :])
            out.append(softmax_step(src_ref, g, v_c, ms[g]))
        return tuple(out)

    def attn_pair(p, ms):
        ms = half_step(lga_ref, lgb_ref, 2 * p, 2 * p + 1, ms)
        return half_step(lgb_ref, lga_ref, 2 * p + 1, jnp.minimum(2 * p + 2, last), ms)

    ms = lax.fori_loop(0, n_chunks // 2, attn_pair,
                       tuple(jnp.full((1, GROUP_ROWS), NEG_BIG, F32) for _ in range(n_groups)))

    @pl.when(n_chunks % 2 == 1)
    def _():
        v_c = vt_ref[0, last]
        for g in range(n_groups):
            softmax_step(lga_ref, g, v_c, ms[g])

    denom = acc_ref[ATT_HEAD_DIM:ATT_HEAD_DIM + 1, :]
    out = (acc_ref[0:ATT_HEAD_DIM, :] / denom).astype(BF16)
    for h in range(ATT_HEADS):
        o_ref[0, :, h * ATT_HEAD_DIM:(h + 1) * ATT_HEAD_DIM] = _dot_nt(
            eye, out[:, h * Q_BLOCK:(h + 1) * Q_BLOCK]).astype(BF16)


def _dsa(q, qi, wi_t, k, v_t, ki):
    bsz, seq, _ = q.shape
    topk = min(TOPK_MAX, seq // 4)
    assert topk % Q_BLOCK == 0 and seq % (2 * KEY_CHUNK) == 0
    assert seq // (PACK_ROWS * COUNT_ACCS) <= BF16_EXACT_INT
    n_blk = seq // Q_BLOCK
    n_ck = seq // KEY_CHUNK
    blk = lambda w: pl.BlockSpec((1, Q_BLOCK, w), lambda b, i: (b, i, 0))
    full = lambda w: pl.BlockSpec((1, seq, w), lambda b, i: (b, 0, 0))
    rows = ATT_HEADS * Q_BLOCK
    return pl.pallas_call(
        functools.partial(_dsa_kernel, topk=topk), grid=(bsz, n_blk),
        in_specs=[blk(ATT_WIDTH), blk(IDX_WIDTH),
                  pl.BlockSpec((1, IDX_HEADS, Q_BLOCK), lambda b, i: (b, 0, i)),
                  full(LANES),
                  pl.BlockSpec((1, n_ck, VT_ROWS, KEY_CHUNK), lambda b, i: (b, 0, 0, 0)),
                  full(IDX_DIM)],
        out_specs=blk(ATT_WIDTH),
        out_shape=jax.ShapeDtypeStruct((bsz, seq, ATT_WIDTH), BF16),
        scratch_shapes=[pltpu.VMEM((rows, QS_COLS), BF16),
                        pltpu.VMEM((IDX_HEADS * Q_BLOCK, IDX_DIM), BF16),
                        pltpu.VMEM((n_ck, KEY_CHUNK, Q_BLOCK), I16),
                        pltpu.VMEM((n_ck, KEY_CHUNK, Q_BLOCK), I16),
                        pltpu.VMEM((n_ck, KEY_CHUNK, Q_BLOCK), I16),
                        pltpu.VMEM((n_ck, KEY_CHUNK, Q_BLOCK), BF16),
                        pltpu.VMEM((2, 1, Q_BLOCK), I32),
                        pltpu.VMEM((2, 1, Q_BLOCK), F32),
                        pltpu.VMEM((VT_ROWS, rows), F32),
                        pltpu.VMEM((KEY_CHUNK, rows), F32),
                        pltpu.VMEM((KEY_CHUNK, rows), F32)],
        compiler_params=pltpu.CompilerParams(dimension_semantics=("arbitrary", "arbitrary"),
                                             vmem_limit_bytes=VMEM_LIMIT),
        name="dsa",
    )(q, qi, wi_t, k, v_t, ki)


def _conf_in_kernel(x_ref, g_ref, w1_ref, b1_ref, cw_ref, cb_ref, lg_ref, lb_ref, o_ref, ubuf_ref, sh_ref):
    s = pl.program_id(1)
    tm = x_ref.shape[1]
    xn = _rms(x_ref[0], g_ref[...]).astype(BF16)
    z = _dot(xn, w1_ref[...]) + b1_ref[...]
    u = z[:, :CONF_WIDTH] * _sigmoid(z[:, CONF_WIDTH:])

    @pl.when(s == 0)
    def _():
        ubuf_ref[0:CONF_HALO, :] = jnp.zeros((CONF_HALO, CONF_WIDTH), F32)

    @pl.when(s > 0)
    def _():
        ubuf_ref[0:CONF_HALO, :] = ubuf_ref[tm:tm + CONF_HALO, :]

    ubuf_ref[CONF_HALO:CONF_HALO + tm, :] = u
    span = tm + CONF_HALO - SUBLANES
    for r in range(1, SUBLANES):
        sh_ref[r - 1, 0:span, :] = ubuf_ref[r:r + span, :]
    conv = u * cw_ref[CONF_KERNEL - 1:CONF_KERNEL, :] + cb_ref[...]
    for j in range(CONF_KERNEL - 1):
        base, r = divmod(CONF_HALO - (CONF_KERNEL - 1) + j, SUBLANES)
        base *= SUBLANES
        win = ubuf_ref[base:base + tm, :] if r == 0 else sh_ref[r - 1, base:base + tm, :]
        conv = conv + win * cw_ref[j:j + 1, :]
    mu = jnp.mean(conv, axis=-1, keepdims=True)
    xc = conv - mu
    var = jnp.mean(xc * xc, axis=-1, keepdims=True)
    y = xc * lax.rsqrt(var + NORM_EPS) * lg_ref[...] + lb_ref[...]
    o_ref[0] = (y * _sigmoid(y)).astype(BF16)


def _conf_in(h, g, w1, b1, conv_w, conv_b, ln_g, ln_b, tm=256):
    bsz, seq, _ = h.shape
    tm = min(tm, seq)
    tok = lambda w: pl.BlockSpec((1, tm, w), lambda b, s: (b, s, 0))
    consts = (g, w1, b1, conv_w, conv_b, ln_g, ln_b)
    return pl.pallas_call(
        _conf_in_kernel, grid=(bsz, seq // tm),
        in_specs=[tok(D_MODEL)] + [_const_spec(c.shape) for c in consts],
        out_specs=tok(CONF_WIDTH),
        out_shape=jax.ShapeDtypeStruct((bsz, seq, CONF_WIDTH), BF16),
        scratch_shapes=[pltpu.VMEM((tm + CONF_HALO, CONF_WIDTH), F32),
                        pltpu.VMEM((SUBLANES - 1, tm + CONF_HALO - SUBLANES, CONF_WIDTH), F32)],
        compiler_params=pltpu.CompilerParams(dimension_semantics=("arbitrary", "arbitrary"),
                                             vmem_limit_bytes=VMEM_LIMIT),
        name="conf_in",
    )(h, *consts)


def _rope_table(seq, head_dim):
    half = head_dim // ROT_FRACTION // 2
    inv_freq = ROPE_THETA ** (-jnp.arange(half, dtype=F32) / half)
    ang = jnp.arange(seq, dtype=F32)[:, None] * inv_freq[None, :]
    cos, sin = jnp.cos(ang), jnp.sin(ang)
    rest = head_dim - 2 * half
    cos_h = jnp.concatenate([cos, cos, jnp.ones((seq, rest), F32)], axis=1)
    sin_h = jnp.concatenate([-sin, sin, jnp.zeros((seq, rest), F32)], axis=1)
    reps = LANES // head_dim
    return jnp.tile(cos_h, (1, reps)), jnp.tile(sin_h, (1, reps))


def _pack_w_in(w_in):
    sizes = (ATT_WIDTH, ATT_HEAD_DIM, ATT_HEAD_DIM, IDX_WIDTH, IDX_DIM, IDX_HEADS)
    offs = [0]
    for sz in sizes:
        offs.append(offs[-1] + sz)
    q, k, v, qi, ki, wi = (w_in[:, offs[j]:offs[j + 1]] for j in range(len(sizes)))
    pad = lambda w: jnp.pad(w, ((0, 0), (0, LANES - w.shape[1])))
    wa = jnp.concatenate([q, qi, pad(k), pad(ki)], axis=1)
    wt = jnp.concatenate([v, wi], axis=1).T
    return wa.astype(BF16), wt.astype(BF16), w_in[:, offs[-1]:].astype(BF16)


def kernel(x, ffn_norm, ffn_w_gate, ffn_w_up, ffn_w_down, mix_norm, hyb_w_in, hyb_conv_w, hyb_w_out,
           conf_w_pw1, conf_b_pw1, conf_conv_w, conf_conv_b, conf_ln_g, conf_ln_b, conf_w_pw2,
           conf_b_pw2, final_norm):
    bsz, seq, _ = x.shape
    n = bsz * seq
    depth = ffn_norm.shape[0]
    row = lambda a: a.reshape(1, -1)
    ffn_w = lambda layer, j: (row(ffn_norm[layer, j]), ffn_w_gate[layer, j].astype(BF16),
                              ffn_w_up[layer, j].astype(BF16), ffn_w_down[layer, j].astype(BF16))
    cos_a, sin_a = _rope_table(seq, ATT_HEAD_DIM)
    cos_i, sin_i = _rope_table(seq, IDX_DIM)
    tab = jnp.concatenate([cos_a, sin_a, cos_i, sin_i], axis=1)

    h = x.reshape(n, D_MODEL)
    for layer in range(depth):
        last = layer == depth - 1
        h = _ffn(h, *ffn_w(layer, 0))
        h3 = h.reshape(bsz, seq, D_MODEL)
        if layer % 2 == 0:
            e = layer // 2
            wa, wt, wb = _pack_w_in(hyb_w_in[e])
            q, qi, k, v_t, ki, wi_t, y_conv = _mix_in(h3, row(mix_norm[layer]), wa, wt, wb,
                                                      hyb_conv_w[e], tab)
            y_attn = _dsa(q, qi, wi_t, k, v_t, ki)
            w_out = hyb_w_out[e].astype(BF16)
            pre = ((y_attn.reshape(n, ATT_WIDTH), w_out[:ATT_WIDTH]),
                   (y_conv.reshape(n, SC_WIDTH), w_out[ATT_WIDTH:]))
            bias = None
        else:
            o = layer // 2
            u = _conf_in(h3, row(mix_norm[layer]), conf_w_pw1[o].astype(BF16), row(conf_b_pw1[o]),
                         conf_conv_w[o], row(conf_conv_b[o]), row(conf_ln_g[o]), row(conf_ln_b[o]))
            pre = ((u.reshape(n, CONF_WIDTH), conf_w_pw2[o].astype(BF16)),)
            bias = row(conf_b_pw2[o])
        h = _ffn(h, *ffn_w(layer, 1), pre=pre, bias=bias,
                 final_g=row(final_norm) if last else None)
    return h.reshape(bsz, seq, D_MODEL)
```

```python
import functools

import jax
import jax.numpy as jnp
from jax import lax
from jax.experimental import pallas as pl
from jax.experimental.pallas import tpu as pltpu

F32 = jnp.float32
BF16 = jnp.bfloat16
I32 = jnp.int32
I16 = jnp.int16

D_MODEL = 1024
D_FF = 2816
FFN_RES_WEIGHT = 0.5
ATT_HEADS = 8
ATT_HEAD_DIM = 64
ATT_WIDTH = ATT_HEADS * ATT_HEAD_DIM
IDX_HEADS = 8
IDX_DIM = 32
IDX_WIDTH = IDX_HEADS * IDX_DIM
TOPK_MAX = 256
Q_BLOCK = 128
SC_WIDTH = D_MODEL - ATT_WIDTH
SC_KERNEL = 3
CONF_WIDTH = D_MODEL
CONF_KERNEL = 31
ROPE_THETA = 500000.0
ROT_FRACTION = 4
NORM_EPS = 1e-6

LANES = 128
SUBLANES = 8
PACK_ROWS = 16
VMEM_LIMIT = 56 * 1024 * 1024

A_Q = 0
A_QI = A_Q + ATT_WIDTH
A_K = A_QI + IDX_WIDTH
A_KI = A_K + LANES
A_COLS = A_KI + LANES

KEY_CHUNK = 512
VT_ROWS = ATT_HEAD_DIM + IDX_HEADS
QS_COLS = 2 * LANES
COUNT_ACCS = 4
COUNT_UNROLL = 4
BF16_EXACT_INT = 256
LO_FIRST_BITS = 11
LOG2E = 1.4426950408889634
GROUP_HEADS = 2
GROUP_ROWS = GROUP_HEADS * Q_BLOCK
NEG_BIG = -1e30
HALF_BIAS = 1 << 15
NEG_INF_KEY = -2 ** 31 + 0x007FFFFF
CONF_HALO = 32
SC_HALO = 8


def _dot(a, b):
    return jnp.dot(a, b, preferred_element_type=F32)


def _dot_nt(a, b):
    return lax.dot_general(a, b, (((1,), (1,)), ((), ())), preferred_element_type=F32)


def _rms(x, g):
    return x * lax.rsqrt(jnp.mean(x * x, axis=-1, keepdims=True) + NORM_EPS) * g


def _sigmoid(x):
    return 1.0 / (1.0 + jnp.exp(-x))


def _ffn_kernel(*refs, n_pre, has_bias, final_norm):
    it = iter(refs)
    x_ref = next(it)
    pre = [(next(it), next(it)) for _ in range(n_pre)]
    bias_ref = next(it) if has_bias else None
    g_ref, wg_ref, wu_ref, wd_ref = next(it), next(it), next(it), next(it)
    gf_ref = next(it) if final_norm else None
    o_ref = next(it)

    x = x_ref[...]
    if n_pre:
        upd = _dot(pre[0][0][...], pre[0][1][...])
        for a_ref, w_ref in pre[1:]:
            upd = upd + _dot(a_ref[...], w_ref[...])
        if has_bias:
            upd = upd + bias_ref[...]
        x = x + upd
    xn = _rms(x, g_ref[...]).astype(BF16)
    gate = _dot(xn, wg_ref[...])
    up = _dot(xn, wu_ref[...])
    mid = (gate * _sigmoid(gate) * up).astype(BF16)
    y = x + FFN_RES_WEIGHT * _dot(mid, wd_ref[...])
    if final_norm:
        y = _rms(y, gf_ref[...])
    o_ref[...] = y


def _const_spec(shape):
    nd = len(shape)
    return pl.BlockSpec(shape, lambda *_: (0,) * nd, pipeline_mode=pl.Buffered(1))


def _ffn(x2d, g, wg, wu, wd, pre=(), bias=None, final_g=None, tm=512):
    n = x2d.shape[0]
    tm = min(tm, n)
    row = lambda w: pl.BlockSpec((tm, w), lambda i: (i, 0))
    args, specs = [x2d], [row(D_MODEL)]
    for act, w in pre:
        args += [act, w]
        specs += [row(act.shape[1]), _const_spec(w.shape)]
    if bias is not None:
        args.append(bias)
        specs.append(_const_spec(bias.shape))
    args += [g, wg, wu, wd]
    specs += [_const_spec(g.shape), _const_spec(wg.shape), _const_spec(wu.shape), _const_spec(wd.shape)]
    if final_g is not None:
        args.append(final_g)
        specs.append(_const_spec(final_g.shape))
    kern = functools.partial(_ffn_kernel, n_pre=len(pre), has_bias=bias is not None,
                             final_norm=final_g is not None)
    return pl.pallas_call(
        kern, grid=(n // tm,), in_specs=specs, out_specs=row(D_MODEL),
        out_shape=jax.ShapeDtypeStruct((n, D_MODEL), F32),
        compiler_params=pltpu.CompilerParams(dimension_semantics=("arbitrary",),
                                             vmem_limit_bytes=VMEM_LIMIT),
        name="ffn",
    )(*args)


def _rope(z, cos, sin, half, period):
    width = z.shape[-1]
    lane = lax.broadcasted_iota(I32, z.shape, 1)
    first = (lane & (period - 1)) < half
    partner = jnp.where(first, pltpu.roll(z, width - half, 1), pltpu.roll(z, half, 1))
    reps = width // LANES
    if reps > 1:
        cos = jnp.concatenate([cos] * reps, axis=1)
        sin = jnp.concatenate([sin] * reps, axis=1)
    return z * cos + partner * sin


def _mix_in_kernel(x_ref, g_ref, wa_ref, wt_ref, wb_ref, cw_ref, tab_ref,
                   q_ref, qi_ref, k_ref, vt_ref, ki_ref, wit_ref, yc_ref,
                   cbuf_ref, carry_ref):
    s = pl.program_id(0)
    b = pl.program_id(1)
    tm = x_ref.shape[1]
    xn = _rms(x_ref[0], g_ref[...]).astype(BF16)
    za = _dot(xn, wa_ref[...])
    cos_a, sin_a = tab_ref[:, 0:LANES], tab_ref[:, LANES:2 * LANES]
    cos_i, sin_i = tab_ref[:, 2 * LANES:3 * LANES], tab_ref[:, 3 * LANES:4 * LANES]
    half_a = ATT_HEAD_DIM // ROT_FRACTION // 2
    half_i = IDX_DIM // ROT_FRACTION // 2

    q = _rope(za[:, A_Q:A_Q + ATT_WIDTH], cos_a, sin_a, half_a, ATT_HEAD_DIM)
    q_ref[0] = (q * (ATT_HEAD_DIM ** -0.5 * LOG2E)).astype(BF16)
    qi = _rope(za[:, A_QI:A_QI + IDX_WIDTH], cos_i, sin_i, half_i, IDX_DIM)
    qi_ref[0] = qi.astype(BF16)
    k = _rope(za[:, A_K:A_K + LANES], cos_a, sin_a, half_a, ATT_HEAD_DIM)
    k_ref[0] = k.astype(BF16)
    ki = _rope(za[:, A_KI:A_KI + LANES], cos_i, sin_i, half_i, IDX_DIM)
    ki_ref[0] = ki[:, :IDX_DIM].astype(BF16)
    zt = _dot_nt(wt_ref[...], xn)
    t_row = lax.broadcasted_iota(I32, zt.shape, 0)
    vt_ref[0, 0] = jnp.where(t_row < ATT_HEAD_DIM, zt,
                             jnp.where(t_row == ATT_HEAD_DIM, 1.0, 0.0)).astype(BF16)
    wit_ref[0] = zt[ATT_HEAD_DIM:, :] * ((IDX_HEADS * IDX_DIM) ** -0.5)

    zb = _dot(xn, wb_ref[...])
    gate_b = zb[:, 0:SC_WIDTH]
    cu = zb[:, SC_WIDTH:2 * SC_WIDTH] * zb[:, 2 * SC_WIDTH:3 * SC_WIDTH]

    @pl.when(s == 0)
    def _():
        cbuf_ref[0:SC_HALO, :] = jnp.zeros((SC_HALO, SC_WIDTH), F32)

    @pl.when(s > 0)
    def _():
        cbuf_ref[0:SC_HALO, :] = carry_ref[b]

    cbuf_ref[SC_HALO:SC_HALO + tm, :] = cu
    carry_ref[b] = cu[tm - SC_HALO:tm, :]
    conv = cu * cw_ref[SC_KERNEL - 1:SC_KERNEL, :]
    for j in range(SC_KERNEL - 1):
        off = SC_HALO - (SC_KERNEL - 1) + j
        conv = conv + cbuf_ref[off:off + tm, :] * cw_ref[j:j + 1, :]
    yc_ref[0] = (gate_b * conv).astype(BF16)


def _mix_in(h, g, wa, wt, wb, conv_w, tab):
    bsz, seq, _ = h.shape
    tm = KEY_CHUNK
    tok = lambda w: pl.BlockSpec((1, tm, w), lambda s, b: (b, s, 0))
    sds = jax.ShapeDtypeStruct
    out_shape = [sds((bsz, seq, ATT_WIDTH), BF16), sds((bsz, seq, IDX_WIDTH), BF16),
                 sds((bsz, seq, LANES), BF16), sds((bsz, seq // tm, VT_ROWS, tm), BF16),
                 sds((bsz, seq, IDX_DIM), BF16), sds((bsz, IDX_HEADS, seq), F32),
                 sds((bsz, seq, SC_WIDTH), BF16)]
    out_specs = [tok(ATT_WIDTH), tok(IDX_WIDTH), tok(LANES),
                 pl.BlockSpec((1, 1, VT_ROWS, tm), lambda s, b: (b, s, 0, 0)),
                 tok(IDX_DIM), pl.BlockSpec((1, IDX_HEADS, tm), lambda s, b: (b, 0, s)),
                 tok(SC_WIDTH)]
    return pl.pallas_call(
        _mix_in_kernel, grid=(seq // tm, bsz),
        in_specs=[tok(D_MODEL), _const_spec(g.shape), _const_spec(wa.shape), _const_spec(wt.shape),
                  _const_spec(wb.shape), _const_spec(conv_w.shape),
                  pl.BlockSpec((tm, 4 * LANES), lambda s, b: (s, 0))],
        out_specs=out_specs, out_shape=out_shape,
        scratch_shapes=[pltpu.VMEM((tm + SC_HALO, SC_WIDTH), F32),
                        pltpu.VMEM((bsz, SC_HALO, SC_WIDTH), F32)],
        compiler_params=pltpu.CompilerParams(dimension_semantics=("arbitrary", "arbitrary"),
                                             vmem_limit_bytes=VMEM_LIMIT),
        name="mix_in",
    )(h, g, wa, wt, wb, conv_w, tab)


def _sortable(x):
    bits = pltpu.bitcast(x + 0.0, I32)
    return bits ^ ((bits >> 31) & 0x7FFFFFFF)


def _count_ge(arr_ref, n_chunks, cand):
    kc = arr_ref.shape[1]
    c16 = cand.astype(I16)
    one, zero = jnp.bfloat16(1), jnp.bfloat16(0)

    def sweep(c, n, accs):
        accs = list(accs)
        for u in range(n):
            for r in range(kc // PACK_ROWS):
                t = arr_ref[c + u, r * PACK_ROWS:(r + 1) * PACK_ROWS, :]
                accs[r % COUNT_ACCS] = accs[r % COUNT_ACCS] + jnp.where(t >= c16, one, zero)
        return tuple(accs)

    accs = tuple(jnp.zeros((PACK_ROWS, Q_BLOCK), BF16) for _ in range(COUNT_ACCS))
    accs = lax.fori_loop(0, n_chunks // COUNT_UNROLL,
                         lambda s, a: sweep(s * COUNT_UNROLL, COUNT_UNROLL, a), accs)
    done = (n_chunks // COUNT_UNROLL) * COUNT_UNROLL
    part = COUNT_UNROLL // 2
    while part:
        take = (n_chunks & part) != 0
        accs = lax.cond(take, functools.partial(sweep, done, part), lambda a: a, accs)
        done = done + jnp.where(take, part, 0)
        part //= 2
    total = accs[0].astype(F32)
    for acc in accs[1:]:
        total = total + acc.astype(F32)
    return jnp.sum(total, axis=0, keepdims=True)


def _descend(arr_ref, n_chunks, base, kf, state, first_bit, n_bits):
    def step(t, carry):
        thr, cge, cgt = carry
        cand = thr | (jnp.int32(1) << (first_bit - t))
        n = base + _count_ge(arr_ref, n_chunks, cand - HALF_BIAS)
        take = n >= kf
        return jnp.where(take, cand, thr), jnp.where(take, n, cge), jnp.where(take, cgt, n)

    return lax.fori_loop(0, n_bits, step, state)


def _dsa_kernel(q_ref, qi_ref, wit_ref, k_ref, vt_ref, ki_ref, o_ref,
                qs_ref, qis_ref, hi_ref, lo_ref, low_ref, bias_ref, thr_ref, need_ref, acc_ref,
                lga_ref, lgb_ref, *, topk):
    i = pl.program_id(1)
    kc = KEY_CHUNK
    rows = ATT_HEADS * Q_BLOCK
    n_chunks = (i * Q_BLOCK + Q_BLOCK + kc - 1) // kc
    n_pairs = (n_chunks + 1) // 2
    q_pos = i * Q_BLOCK + lax.broadcasted_iota(I32, (kc, Q_BLOCK), 1)
    kf = jnp.float32(topk)

    r_i = lax.broadcasted_iota(I32, (Q_BLOCK, Q_BLOCK), 0)
    c_i = lax.broadcasted_iota(I32, (Q_BLOCK, Q_BLOCK), 1)
    eye = jnp.where(r_i == c_i, 1.0, 0.0).astype(BF16)
    for h in range(ATT_HEADS):
        blk_rows = slice(h * Q_BLOCK, (h + 1) * Q_BLOCK)
        qs_ref[blk_rows, 0:LANES] = jnp.zeros((Q_BLOCK, LANES), BF16)
        qs_ref[blk_rows, 0:ATT_HEAD_DIM] = q_ref[0, :, h * ATT_HEAD_DIM:(h + 1) * ATT_HEAD_DIM]
        qs_ref[blk_rows, LANES:QS_COLS] = eye
    for h in range(IDX_HEADS):
        qis_ref[h * Q_BLOCK:(h + 1) * Q_BLOCK, :] = qi_ref[0, :, h * IDX_DIM:(h + 1) * IDX_DIM]

    wit = wit_ref[0]

    def score_chunk(c, diagonal):
        start = pl.multiple_of(c * kc, kc)
        ki_c = ki_ref[0, pl.ds(start, kc), :]
        score = jnp.zeros((kc, Q_BLOCK), F32)
        for g in range(IDX_HEADS // GROUP_HEADS):
            dots = _dot_nt(ki_c, qis_ref[g * GROUP_ROWS:(g + 1) * GROUP_ROWS, :])
            for j in range(GROUP_HEADS):
                h = g * GROUP_HEADS + j
                score = score + wit[h:h + 1, :] * jnp.maximum(dots[:, j * Q_BLOCK:(j + 1) * Q_BLOCK], 0.0)
        if diagonal:
            k_pos = start + lax.broadcasted_iota(I32, (kc, Q_BLOCK), 0)
            score = jnp.where(k_pos <= q_pos, score, -jnp.inf)
        key = _sortable(score)
        hi_ref[c] = (key >> 16).astype(I16)
        lo_ref[c] = ((key & 0xFFFF) - HALF_BIAS).astype(I16)

    def score_pair(p, carry):
        score_chunk(2 * p, False)
        score_chunk(2 * p + 1, False)
        return carry

    lax.fori_loop(0, n_pairs - 1, score_pair, 0)
    score_chunk(2 * n_pairs - 2, True)

    @pl.when(n_chunks % 2 == 0)
    def _():
        score_chunk(2 * n_pairs - 1, True)

    @pl.when((i + 1) * Q_BLOCK <= topk)
    def _():
        above = NEG_INF_KEY + 1
        thr_ref[0] = jnp.full((1, Q_BLOCK), above >> 16, I32)
        thr_ref[1] = jnp.full((1, Q_BLOCK), (above & 0xFFFF) - HALF_BIAS, I32)
        need_ref[0] = jnp.zeros((1, Q_BLOCK), F32)
        need_ref[1] = jnp.zeros((1, Q_BLOCK), F32)

    @pl.when((i + 1) * Q_BLOCK > topk)
    def _():
        total = jnp.full((1, Q_BLOCK), n_chunks * kc, I32).astype(F32)
        zero = jnp.zeros((1, Q_BLOCK), F32)
        zero_i = jnp.zeros((1, Q_BLOCK), I32)

        t_hi, cge_hi, cgt_hi = _descend(hi_ref, n_chunks, zero, kf, (zero_i, total, zero), 15, 16)
        t_hi = t_hi - HALF_BIAS
        t_hi16 = t_hi.astype(I16)

        def bucket_chunk(c, carry):
            low_ref[c] = jnp.where(hi_ref[c] == t_hi16, lo_ref[c], jnp.int16(-HALF_BIAS))
            return carry

        lax.fori_loop(0, n_chunks, bucket_chunk, 0)

        t_lo, cge, cgt = _descend(low_ref, n_chunks, cgt_hi, kf, (zero_i, cge_hi, cgt_hi),
                                  15, LO_FIRST_BITS)
        thr_ref[1] = t_lo
        need_ref[0] = cge
        need_ref[1] = cgt

        @pl.when(jnp.max(jnp.abs(cge - kf)) > 0.0)
        def _():
            t, c_ge, c_gt = _descend(low_ref, n_chunks, cgt_hi, kf,
                                     (thr_ref[1], need_ref[0], need_ref[1]),
                                     15 - LO_FIRST_BITS, 16 - LO_FIRST_BITS)
            thr_ref[1] = t
            need_ref[0] = c_ge
            need_ref[1] = c_gt

        cge, cgt = need_ref[0], need_ref[1]
        thr_ref[0] = t_hi
        thr_ref[1] = thr_ref[1] - HALF_BIAS
        need_ref[0] = kf - cgt
        need_ref[1] = cge - cgt

    t_hi16 = thr_ref[0].astype(I16)
    t_lo16 = thr_ref[1].astype(I16)

    def bias_chunk(c, carry):
        hi, lo = hi_ref[c], lo_ref[c]
        sel = (hi > t_hi16) | ((hi == t_hi16) & (lo >= t_lo16))
        bias_ref[c] = jnp.where(sel, jnp.bfloat16(0.0), jnp.bfloat16(NEG_BIG))
        return carry

    lax.fori_loop(0, n_chunks, bias_chunk, 0)

    @pl.when(jnp.max(need_ref[1] - need_ref[0]) > 0.0)
    def _():
        need = need_ref[0]
        r_i = lax.broadcasted_iota(I32, (kc, kc), 0)
        c_i = lax.broadcasted_iota(I32, (kc, kc), 1)
        tri = jnp.where(c_i <= r_i, 1.0, 0.0).astype(BF16)

        def tie_chunk(c, seen):
            eq = (hi_ref[c] == t_hi16) & (lo_ref[c] == t_lo16)
            eq_b = jnp.where(eq, jnp.bfloat16(1.0), jnp.bfloat16(0.0))
            rank = seen + _dot(tri, eq_b)
            drop = (eq_b.astype(F32) > 0.5) & (rank > need)
            bias_ref[c] = jnp.where(drop, NEG_BIG, bias_ref[c].astype(F32)).astype(BF16)
            return rank[kc - 1:kc, :]

        lax.fori_loop(0, n_chunks, tie_chunk, jnp.zeros((1, Q_BLOCK), F32))

    acc_ref[...] = jnp.zeros(acc_ref.shape, F32)

    n_groups = ATT_HEADS // GROUP_HEADS
    group = lambda g: slice(g * GROUP_ROWS, (g + 1) * GROUP_ROWS)

    def keys_of(c):
        start = pl.multiple_of(c * kc, kc)
        return jnp.concatenate([k_ref[0, pl.ds(start, kc), :], bias_ref[c]], axis=1)

    def softmax_step(lg_ref, g, v_c, m):
        lg = lg_ref[:, group(g)]
        m_new = jnp.maximum(m, jnp.max(lg, axis=0, keepdims=True))
        p = jnp.exp2(lg - m_new)
        acc_ref[:, group(g)] = jnp.exp2(m - m_new) * acc_ref[:, group(g)] + _dot(v_c, p.astype(BF16))
        return m_new

    last = n_chunks - 1
    lga_ref[...] = _dot_nt(keys_of(0), qs_ref[...])

    def half_step(src_ref, dst_ref, c_src, c_dst, ms):
        keys, v_c = keys_of(c_dst), vt_ref[0, c_src]
        out = []
        for g in range(n_groups):
            dst_ref[:, group(g)] = _dot_nt(keys, qs_ref[group(g), :])
            out.append(softmax_step(src_ref, g, v_c, ms[g]))
        return tuple(out)

    def attn_pair(p, ms):
        ms = half_step(lga_ref, lgb_ref, 2 * p, 2 * p + 1, ms)
        return half_step(lgb_ref, lga_ref, 2 * p + 1, jnp.minimum(2 * p + 2, last), ms)

    ms = lax.fori_loop(0, n_chunks // 2, attn_pair,
                       tuple(jnp.full((1, GROUP_ROWS), NEG_BIG, F32) for _ in range(n_groups)))

    @pl.when(n_chunks % 2 == 1)
    def _():
        v_c = vt_ref[0, last]
        for g in range(n_groups):
            softmax_step(lga_ref, g, v_c, ms[g])

    denom = acc_ref[ATT_HEAD_DIM:ATT_HEAD_DIM + 1, :]
    out = (acc_ref[0:ATT_HEAD_DIM, :] / denom).astype(BF16)
    for h in range(ATT_HEADS):
        o_ref[0, :, h * ATT_HEAD_DIM:(h + 1) * ATT_HEAD_DIM] = _dot_nt(
            eye, out[:, h * Q_BLOCK:(h + 1) * Q_BLOCK]).astype(BF16)


def _dsa(q, qi, wi_t, k, v_t, ki):
    bsz, seq, _ = q.shape
    topk = min(TOPK_MAX, seq // 4)
    assert topk % Q_BLOCK == 0 and seq % (2 * KEY_CHUNK) == 0
    assert seq // (PACK_ROWS * COUNT_ACCS) <= BF16_EXACT_INT
    n_blk = seq // Q_BLOCK
    n_ck = seq // KEY_CHUNK
    blk = lambda w: pl.BlockSpec((1, Q_BLOCK, w), lambda b, i: (b, i, 0))
    full = lambda w: pl.BlockSpec((1, seq, w), lambda b, i: (b, 0, 0))
    rows = ATT_HEADS * Q_BLOCK
    return pl.pallas_call(
        functools.partial(_dsa_kernel, topk=topk), grid=(bsz, n_blk),
        in_specs=[blk(ATT_WIDTH), blk(IDX_WIDTH),
                  pl.BlockSpec((1, IDX_HEADS, Q_BLOCK), lambda b, i: (b, 0, i)),
                  full(LANES),
                  pl.BlockSpec((1, n_ck, VT_ROWS, KEY_CHUNK), lambda b, i: (b, 0, 0, 0)),
                  full(IDX_DIM)],
        out_specs=blk(ATT_WIDTH),
        out_shape=jax.ShapeDtypeStruct((bsz, seq, ATT_WIDTH), BF16),
        scratch_shapes=[pltpu.VMEM((rows, QS_COLS), BF16),
                        pltpu.VMEM((IDX_HEADS * Q_BLOCK, IDX_DIM), BF16),
                        pltpu.VMEM((n_ck, KEY_CHUNK, Q_BLOCK), I16),
                        pltpu.VMEM((n_ck, KEY_CHUNK, Q_BLOCK), I16),
                        pltpu.VMEM((n_ck, KEY_CHUNK, Q_BLOCK), I16),
                        pltpu.VMEM((n_ck, KEY_CHUNK, Q_BLOCK), BF16),
                        pltpu.VMEM((2, 1, Q_BLOCK), I32),
                        pltpu.VMEM((2, 1, Q_BLOCK), F32),
                        pltpu.VMEM((VT_ROWS, rows), F32),
                        pltpu.VMEM((KEY_CHUNK, rows), F32),
                        pltpu.VMEM((KEY_CHUNK, rows), F32)],
        compiler_params=pltpu.CompilerParams(dimension_semantics=("arbitrary", "arbitrary"),
                                             vmem_limit_bytes=VMEM_LIMIT),
        name="dsa",
    )(q, qi, wi_t, k, v_t, ki)


def _conf_in_kernel(x_ref, g_ref, w1_ref, b1_ref, cw_ref, cb_ref, lg_ref, lb_ref, o_ref, ubuf_ref, sh_ref):
    s = pl.program_id(1)
    tm = x_ref.shape[1]
    xn = _rms(x_ref[0], g_ref[...]).astype(BF16)
    z = _dot(xn, w1_ref[...]) + b1_ref[...]
    u = z[:, :CONF_WIDTH] * _sigmoid(z[:, CONF_WIDTH:])

    @pl.when(s == 0)
    def _():
        ubuf_ref[0:CONF_HALO, :] = jnp.zeros((CONF_HALO, CONF_WIDTH), F32)

    @pl.when(s > 0)
    def _():
        ubuf_ref[0:CONF_HALO, :] = ubuf_ref[tm:tm + CONF_HALO, :]

    ubuf_ref[CONF_HALO:CONF_HALO + tm, :] = u
    span = tm + CONF_HALO - SUBLANES
    for r in range(1, SUBLANES):
        sh_ref[r - 1, 0:span, :] = ubuf_ref[r:r + span, :]
    conv = u * cw_ref[CONF_KERNEL - 1:CONF_KERNEL, :] + cb_ref[...]
    for j in range(CONF_KERNEL - 1):
        base, r = divmod(CONF_HALO - (CONF_KERNEL - 1) + j, SUBLANES)
        base *= SUBLANES
        win = ubuf_ref[base:base + tm, :] if r == 0 else sh_ref[r - 1, base:base + tm, :]
        conv = conv + win * cw_ref[j:j + 1, :]
    mu = jnp.mean(conv, axis=-1, keepdims=True)
    xc = conv - mu
    var = jnp.mean(xc * xc, axis=-1, keepdims=True)
    y = xc * lax.rsqrt(var + NORM_EPS) * lg_ref[...] + lb_ref[...]
    o_ref[0] = (y * _sigmoid(y)).astype(BF16)


def _conf_in(h, g, w1, b1, conv_w, conv_b, ln_g, ln_b, tm=256):
    bsz, seq, _ = h.shape
    tm = min(tm, seq)
    tok = lambda w: pl.BlockSpec((1, tm, w), lambda b, s: (b, s, 0))
    consts = (g, w1, b1, conv_w, conv_b, ln_g, ln_b)
    return pl.pallas_call(
        _conf_in_kernel, grid=(bsz, seq // tm),
        in_specs=[tok(D_MODEL)] + [_const_spec(c.shape) for c in consts],
        out_specs=tok(CONF_WIDTH),
        out_shape=jax.ShapeDtypeStruct((bsz, seq, CONF_WIDTH), BF16),
        scratch_shapes=[pltpu.VMEM((tm + CONF_HALO, CONF_WIDTH), F32),
                        pltpu.VMEM((SUBLANES - 1, tm + CONF_HALO - SUBLANES, CONF_WIDTH), F32)],
        compiler_params=pltpu.CompilerParams(dimension_semantics=("arbitrary", "arbitrary"),
                                             vmem_limit_bytes=VMEM_LIMIT),
        name="conf_in",
    )(h, *consts)


def _rope_table(seq, head_dim):
    half = head_dim // ROT_FRACTION // 2
    inv_freq = ROPE_THETA ** (-jnp.arange(half, dtype=F32) / half)
    ang = jnp.arange(seq, dtype=F32)[:, None] * inv_freq[None, :]
    cos, sin = jnp.cos(ang), jnp.sin(ang)
    rest = head_dim - 2 * half
    cos_h = jnp.concatenate([cos, cos, jnp.ones((seq, rest), F32)], axis=1)
    sin_h = jnp.concatenate([-sin, sin, jnp.zeros((seq, rest), F32)], axis=1)
    reps = LANES // head_dim
    return jnp.tile(cos_h, (1, reps)), jnp.tile(sin_h, (1, reps))


def _pack_w_in(w_in):
    sizes = (ATT_WIDTH, ATT_HEAD_DIM, ATT_HEAD_DIM, IDX_WIDTH, IDX_DIM, IDX_HEADS)
    offs = [0]
    for sz in sizes:
        offs.append(offs[-1] + sz)
    q, k, v, qi, ki, wi = (w_in[:, offs[j]:offs[j + 1]] for j in range(len(sizes)))
    pad = lambda w: jnp.pad(w, ((0, 0), (0, LANES - w.shape[1])))
    wa = jnp.concatenate([q, qi, pad(k), pad(ki)], axis=1)
    wt = jnp.concatenate([v, wi], axis=1).T
    return wa.astype(BF16), wt.astype(BF16), w_in[:, offs[-1]:].astype(BF16)


def kernel(x, ffn_norm, ffn_w_gate, ffn_w_up, ffn_w_down, mix_norm, hyb_w_in, hyb_conv_w, hyb_w_out,
           conf_w_pw1, conf_b_pw1, conf_conv_w, conf_conv_b, conf_ln_g, conf_ln_b, conf_w_pw2,
           conf_b_pw2, final_norm):
    bsz, seq, _ = x.shape
    n = bsz * seq
    depth = ffn_norm.shape[0]
    row = lambda a: a.reshape(1, -1)
    ffn_w = lambda layer, j: (row(ffn_norm[layer, j]), ffn_w_gate[layer, j].astype(BF16),
                              ffn_w_up[layer, j].astype(BF16), ffn_w_down[layer, j].astype(BF16))
    cos_a, sin_a = _rope_table(seq, ATT_HEAD_DIM)
    cos_i, sin_i = _rope_table(seq, IDX_DIM)
    tab = jnp.concatenate([cos_a, sin_a, cos_i, sin_i], axis=1)

    h = x.reshape(n, D_MODEL)
    for layer in range(depth):
        last = layer == depth - 1
        h = _ffn(h, *ffn_w(layer, 0))
        h3 = h.reshape(bsz, seq, D_MODEL)
        if layer % 2 == 0:
            e = layer // 2
            wa, wt, wb = _pack_w_in(hyb_w_in[e])
            q, qi, k, v_t, ki, wi_t, y_conv = _mix_in(h3, row(mix_norm[layer]), wa, wt, wb,
                                                      hyb_conv_w[e], tab)
            y_attn = _dsa(q, qi, wi_t, k, v_t, ki)
            w_out = hyb_w_out[e].astype(BF16)
            pre = ((y_attn.reshape(n, ATT_WIDTH), w_out[:ATT_WIDTH]),
                   (y_conv.reshape(n, SC_WIDTH), w_out[ATT_WIDTH:]))
            bias = None
        else:
            o = layer // 2
            u = _conf_in(h3, row(mix_norm[layer]), conf_w_pw1[o].astype(BF16), row(conf_b_pw1[o]),
                         conf_conv_w[o], row(conf_conv_b[o]), row(conf_ln_g[o]), row(conf_ln_b[o]))
            pre = ((u.reshape(n, CONF_WIDTH), conf_w_pw2[o].astype(BF16)),)
            bias = row(conf_b_pw2[o])
        h = _ffn(h, *ffn_w(layer, 1), pre=pre, bias=bias,
                 final_g=row(final_norm) if last else None)
    return h.reshape(bsz, seq, D_MODEL)
```

```python
import functools

import jax
import jax.numpy as jnp
from jax import lax
from jax.experimental import pallas as pl
from jax.experimental.pallas import tpu as pltpu

F32 = jnp.float32
BF16 = jnp.bfloat16
I32 = jnp.int32
I16 = jnp.int16

D_MODEL = 1024
D_FF = 2816
FFN_RES_WEIGHT = 0.5
ATT_HEADS = 8
ATT_HEAD_DIM = 64
ATT_WIDTH = ATT_HEADS * ATT_HEAD_DIM
IDX_HEADS = 8
IDX_DIM = 32
IDX_WIDTH = IDX_HEADS * IDX_DIM
TOPK_MAX = 256
Q_BLOCK = 128
SC_WIDTH = D_MODEL - ATT_WIDTH
SC_KERNEL = 3
CONF_WIDTH = D_MODEL
CONF_KERNEL = 31
ROPE_THETA = 500000.0
ROT_FRACTION = 4
NORM_EPS = 1e-6

LANES = 128
SUBLANES = 8
PACK_ROWS = 16
VMEM_LIMIT = 56 * 1024 * 1024

A_Q = 0
A_QI = A_Q + ATT_WIDTH
A_K = A_QI + IDX_WIDTH
A_KI = A_K + LANES
A_COLS = A_KI + LANES

KEY_CHUNK = 512
VT_ROWS = ATT_HEAD_DIM + IDX_HEADS
QS_COLS = 2 * LANES
COUNT_ACCS = 4
COUNT_UNROLL = 4
BF16_EXACT_INT = 256
LO_FIRST_BITS = 11
LOG2E = 1.4426950408889634
GROUP_HEADS = 2
GROUP_ROWS = GROUP_HEADS * Q_BLOCK
NEG_BIG = -1e30
HALF_BIAS = 1 << 15
NEG_INF_KEY = -2 ** 31 + 0x007FFFFF
CONF_HALO = 32
SC_HALO = 8


def _dot(a, b):
    return jnp.dot(a, b, preferred_element_type=F32)


def _dot_nt(a, b):
    return lax.dot_general(a, b, (((1,), (1,)), ((), ())), preferred_element_type=F32)


def _rms(x, g):
    return x * lax.rsqrt(jnp.mean(x * x, axis=-1, keepdims=True) + NORM_EPS) * g


def _sigmoid(x):
    return 1.0 / (1.0 + jnp.exp(-x))


def _ffn_kernel(*refs, n_pre, has_bias, final_norm):
    it = iter(refs)
    x_ref = next(it)
    pre = [(next(it), next(it)) for _ in range(n_pre)]
    bias_ref = next(it) if has_bias else None
    g_ref, wg_ref, wu_ref, wd_ref = next(it), next(it), next(it), next(it)
    gf_ref = next(it) if final_norm else None
    o_ref = next(it)

    x = x_ref[...]
    if n_pre:
        upd = _dot(pre[0][0][...], pre[0][1][...])
        for a_ref, w_ref in pre[1:]:
            upd = upd + _dot(a_ref[...], w_ref[...])
        if has_bias:
            upd = upd + bias_ref[...]
        x = x + upd
    xn = _rms(x, g_ref[...]).astype(BF16)
    gate = _dot(xn, wg_ref[...])
    up = _dot(xn, wu_ref[...])
    mid = (gate * _sigmoid(gate) * up).astype(BF16)
    y = x + FFN_RES_WEIGHT * _dot(mid, wd_ref[...])
    if final_norm:
        y = _rms(y, gf_ref[...])
    o_ref[...] = y


def _const_spec(shape):
    nd = len(shape)
    return pl.BlockSpec(shape, lambda *_: (0,) * nd, pipeline_mode=pl.Buffered(1))


def _ffn(x2d, g, wg, wu, wd, pre=(), bias=None, final_g=None, tm=512):
    n = x2d.shape[0]
    tm = min(tm, n)
    row = lambda w: pl.BlockSpec((tm, w), lambda i: (i, 0))
    args, specs = [x2d], [row(D_MODEL)]
    for act, w in pre:
        args += [act, w]
        specs += [row(act.shape[1]), _const_spec(w.shape)]
    if bias is not None:
        args.append(bias)
        specs.append(_const_spec(bias.shape))
    args += [g, wg, wu, wd]
    specs += [_const_spec(g.shape), _const_spec(wg.shape), _const_spec(wu.shape), _const_spec(wd.shape)]
    if final_g is not None:
        args.append(final_g)
        specs.append(_const_spec(final_g.shape))
    kern = functools.partial(_ffn_kernel, n_pre=len(pre), has_bias=bias is not None,
                             final_norm=final_g is not None)
    return pl.pallas_call(
        kern, grid=(n // tm,), in_specs=specs, out_specs=row(D_MODEL),
        out_shape=jax.ShapeDtypeStruct((n, D_MODEL), F32),
        compiler_params=pltpu.CompilerParams(dimension_semantics=("arbitrary",),
                                             vmem_limit_bytes=VMEM_LIMIT),
        name="ffn",
    )(*args)


def _rope(z, cos, sin, half, period):
    width = z.shape[-1]
    lane = lax.broadcasted_iota(I32, z.shape, 1)
    first = (lane & (period - 1)) < half
    partner = jnp.where(first, pltpu.roll(z, width - half, 1), pltpu.roll(z, half, 1))
    reps = width // LANES
    if reps > 1:
        cos = jnp.concatenate([cos] * reps, axis=1)
        sin = jnp.concatenate([sin] * reps, axis=1)
    return z * cos + partner * sin


def _mix_in_kernel(x_ref, g_ref, wa_ref, wt_ref, wb_ref, cw_ref, tab_ref,
                   q_ref, qi_ref, k_ref, vt_ref, ki_ref, wit_ref, yc_ref,
                   cbuf_ref, carry_ref):
    s = pl.program_id(0)
    b = pl.program_id(1)
    tm = x_ref.shape[1]
    xn = _rms(x_ref[0], g_ref[...]).astype(BF16)
    za = _dot(xn, wa_ref[...])
    cos_a, sin_a = tab_ref[:, 0:LANES], tab_ref[:, LANES:2 * LANES]
    cos_i, sin_i = tab_ref[:, 2 * LANES:3 * LANES], tab_ref[:, 3 * LANES:4 * LANES]
    half_a = ATT_HEAD_DIM // ROT_FRACTION // 2
    half_i = IDX_DIM // ROT_FRACTION // 2

    q = _rope(za[:, A_Q:A_Q + ATT_WIDTH], cos_a, sin_a, half_a, ATT_HEAD_DIM)
    q_ref[0] = (q * (ATT_HEAD_DIM ** -0.5 * LOG2E)).astype(BF16)
    qi = _rope(za[:, A_QI:A_QI + IDX_WIDTH], cos_i, sin_i, half_i, IDX_DIM)
    qi_ref[0] = qi.astype(BF16)
    k = _rope(za[:, A_K:A_K + LANES], cos_a, sin_a, half_a, ATT_HEAD_DIM)
    k_ref[0] = k.astype(BF16)
    ki = _rope(za[:, A_KI:A_KI + LANES], cos_i, sin_i, half_i, IDX_DIM)
    ki_ref[0] = ki[:, :IDX_DIM].astype(BF16)
    zt = _dot_nt(wt_ref[...], xn)
    t_row = lax.broadcasted_iota(I32, zt.shape, 0)
    vt_ref[0, 0] = jnp.where(t_row < ATT_HEAD_DIM, zt,
                             jnp.where(t_row == ATT_HEAD_DIM, 1.0, 0.0)).astype(BF16)
    wit_ref[0] = zt[ATT_HEAD_DIM:, :] * ((IDX_HEADS * IDX_DIM) ** -0.5)

    zb = _dot(xn, wb_ref[...])
    gate_b = zb[:, 0:SC_WIDTH]
    cu = zb[:, SC_WIDTH:2 * SC_WIDTH] * zb[:, 2 * SC_WIDTH:3 * SC_WIDTH]

    @pl.when(s == 0)
    def _():
        cbuf_ref[0:SC_HALO, :] = jnp.zeros((SC_HALO, SC_WIDTH), F32)

    @pl.when(s > 0)
    def _():
        cbuf_ref[0:SC_HALO, :] = carry_ref[b]

    cbuf_ref[SC_HALO:SC_HALO + tm, :] = cu
    carry_ref[b] = cu[tm - SC_HALO:tm, :]
    conv = cu * cw_ref[SC_KERNEL - 1:SC_KERNEL, :]
    for j in range(SC_KERNEL - 1):
        off = SC_HALO - (SC_KERNEL - 1) + j
        conv = conv + cbuf_ref[off:off + tm, :] * cw_ref[j:j + 1, :]
    yc_ref[0] = (gate_b * conv).astype(BF16)


def _mix_in(h, g, wa, wt, wb, conv_w, tab):
    bsz, seq, _ = h.shape
    tm = KEY_CHUNK
    tok = lambda w: pl.BlockSpec((1, tm, w), lambda s, b: (b, s, 0))
    sds = jax.ShapeDtypeStruct
    out_shape = [sds((bsz, seq, ATT_WIDTH), BF16), sds((bsz, seq, IDX_WIDTH), BF16),
                 sds((bsz, seq, LANES), BF16), sds((bsz, seq // tm, VT_ROWS, tm), BF16),
                 sds((bsz, seq, IDX_DIM), BF16), sds((bsz, IDX_HEADS, seq), F32),
                 sds((bsz, seq, SC_WIDTH), BF16)]
    out_specs = [tok(ATT_WIDTH), tok(IDX_WIDTH), tok(LANES),
                 pl.BlockSpec((1, 1, VT_ROWS, tm), lambda s, b: (b, s, 0, 0)),
                 tok(IDX_DIM), pl.BlockSpec((1, IDX_HEADS, tm), lambda s, b: (b, 0, s)),
                 tok(SC_WIDTH)]
    return pl.pallas_call(
        _mix_in_kernel, grid=(seq // tm, bsz),
        in_specs=[tok(D_MODEL), _const_spec(g.shape), _const_spec(wa.shape), _const_spec(wt.shape),
                  _const_spec(wb.shape), _const_spec(conv_w.shape),
                  pl.BlockSpec((tm, 4 * LANES), lambda s, b: (s, 0))],
        out_specs=out_specs, out_shape=out_shape,
        scratch_shapes=[pltpu.VMEM((tm + SC_HALO, SC_WIDTH), F32),
                        pltpu.VMEM((bsz, SC_HALO, SC_WIDTH), F32)],
        compiler_params=pltpu.CompilerParams(dimension_semantics=("arbitrary", "arbitrary"),
                                             vmem_limit_bytes=VMEM_LIMIT),
        name="mix_in",
    )(h, g, wa, wt, wb, conv_w, tab)


def _sortable(x):
    bits = pltpu.bitcast(x + 0.0, I32)
    return bits ^ ((bits >> 31) & 0x7FFFFFFF)


def _count_ge(arr_ref, n_chunks, cand):
    kc = arr_ref.shape[1]
    c16 = cand.astype(I16)
    one, zero = jnp.bfloat16(1), jnp.bfloat16(0)

    def sweep(c, n, accs):
        accs = list(accs)
        for u in range(n):
            for r in range(kc // PACK_ROWS):
                t = arr_ref[c + u, r * PACK_ROWS:(r + 1) * PACK_ROWS, :]
                accs[r % COUNT_ACCS] = accs[r % COUNT_ACCS] + jnp.where(t >= c16, one, zero)
        return tuple(accs)

    accs = tuple(jnp.zeros((PACK_ROWS, Q_BLOCK), BF16) for _ in range(COUNT_ACCS))
    accs = lax.fori_loop(0, n_chunks // COUNT_UNROLL,
                         lambda s, a: sweep(s * COUNT_UNROLL, COUNT_UNROLL, a), accs)
    done = (n_chunks // COUNT_UNROLL) * COUNT_UNROLL
    part = COUNT_UNROLL // 2
    while part:
        take = (n_chunks & part) != 0
        accs = lax.cond(take, functools.partial(sweep, done, part), lambda a: a, accs)
        done = done + jnp.where(take, part, 0)
        part //= 2
    total = accs[0].astype(F32)
    for acc in accs[1:]:
        total = total + acc.astype(F32)
    return jnp.sum(total, axis=0, keepdims=True)


def _descend(arr_ref, n_chunks, base, kf, state, first_bit, n_bits):
    def step(t, carry):
        thr, cge, cgt = carry
        cand = thr | (jnp.int32(1) << (first_bit - t))
        n = base + _count_ge(arr_ref, n_chunks, cand - HALF_BIAS)
        take = n >= kf
        return jnp.where(take, cand, thr), jnp.where(take, n, cge), jnp.where(take, cgt, n)

    return lax.fori_loop(0, n_bits, step, state)


def _dsa_kernel(q_ref, qi_ref, wit_ref, k_ref, vt_ref, ki_ref, o_ref,
                qs_ref, qis_ref, hi_ref, lo_ref, low_ref, thr_ref, need_ref, acc_ref,
                lga_ref, lgb_ref, *, topk):
    i = pl.program_id(1)
    kc = KEY_CHUNK
    rows = ATT_HEADS * Q_BLOCK
    n_chunks = (i * Q_BLOCK + Q_BLOCK + kc - 1) // kc
    n_pairs = (n_chunks + 1) // 2
    q_pos = i * Q_BLOCK + lax.broadcasted_iota(I32, (kc, Q_BLOCK), 1)
    kf = jnp.float32(topk)

    r_i = lax.broadcasted_iota(I32, (Q_BLOCK, Q_BLOCK), 0)
    c_i = lax.broadcasted_iota(I32, (Q_BLOCK, Q_BLOCK), 1)
    eye = jnp.where(r_i == c_i, 1.0, 0.0).astype(BF16)
    for h in range(ATT_HEADS):
        blk_rows = slice(h * Q_BLOCK, (h + 1) * Q_BLOCK)
        qs_ref[blk_rows, 0:LANES] = jnp.zeros((Q_BLOCK, LANES), BF16)
        qs_ref[blk_rows, 0:ATT_HEAD_DIM] = q_ref[0, :, h * ATT_HEAD_DIM:(h + 1) * ATT_HEAD_DIM]
        qs_ref[blk_rows, LANES:QS_COLS] = eye
    for h in range(IDX_HEADS):
        qis_ref[h * Q_BLOCK:(h + 1) * Q_BLOCK, :] = qi_ref[0, :, h * IDX_DIM:(h + 1) * IDX_DIM]

    wit = wit_ref[0]

    def score_chunk(c, diagonal):
        start = pl.multiple_of(c * kc, kc)
        ki_c = ki_ref[0, pl.ds(start, kc), :]
        score = jnp.zeros((kc, Q_BLOCK), F32)
        for g in range(IDX_HEADS // GROUP_HEADS):
            dots = _dot_nt(ki_c, qis_ref[g * GROUP_ROWS:(g + 1) * GROUP_ROWS, :])
            for j in range(GROUP_HEADS):
                h = g * GROUP_HEADS + j
                score = score + wit[h:h + 1, :] * jnp.maximum(dots[:, j * Q_BLOCK:(j + 1) * Q_BLOCK], 0.0)
        if diagonal:
            k_pos = start + lax.broadcasted_iota(I32, (kc, Q_BLOCK), 0)
            score = jnp.where(k_pos <= q_pos, score, -jnp.inf)
        key = _sortable(score)
        hi_ref[c] = (key >> 16).astype(I16)
        lo_ref[c] = ((key & 0xFFFF) - HALF_BIAS).astype(I16)

    def score_pair(p, carry):
        score_chunk(2 * p, False)
        score_chunk(2 * p + 1, False)
        return carry

    lax.fori_loop(0, n_pairs - 1, score_pair, 0)
    score_chunk(2 * n_pairs - 2, True)

    @pl.when(n_chunks % 2 == 0)
    def _():
        score_chunk(2 * n_pairs - 1, True)

    @pl.when((i + 1) * Q_BLOCK <= topk)
    def _():
        above = NEG_INF_KEY + 1
        thr_ref[0] = jnp.full((1, Q_BLOCK), above >> 16, I32)
        thr_ref[1] = jnp.full((1, Q_BLOCK), (above & 0xFFFF) - HALF_BIAS, I32)
        need_ref[0] = jnp.zeros((1, Q_BLOCK), F32)
        need_ref[1] = jnp.zeros((1, Q_BLOCK), F32)

    @pl.when((i + 1) * Q_BLOCK > topk)
    def _():
        total = jnp.full((1, Q_BLOCK), n_chunks * kc, I32).astype(F32)
        zero = jnp.zeros((1, Q_BLOCK), F32)
        zero_i = jnp.zeros((1, Q_BLOCK), I32)

        t_hi, cge_hi, cgt_hi = _descend(hi_ref, n_chunks, zero, kf, (zero_i, total, zero), 15, 16)
        t_hi = t_hi - HALF_BIAS
        t_hi16 = t_hi.astype(I16)

        def bucket_chunk(c, carry):
            low_ref[c] = jnp.where(hi_ref[c] == t_hi16, lo_ref[c], jnp.int16(-HALF_BIAS))
            return carry

        lax.fori_loop(0, n_chunks, bucket_chunk, 0)

        t_lo, cge, cgt = _descend(low_ref, n_chunks, cgt_hi, kf, (zero_i, cge_hi, cgt_hi),
                                  15, LO_FIRST_BITS)
        thr_ref[1] = t_lo
        need_ref[0] = cge
        need_ref[1] = cgt

        @pl.when(jnp.max(jnp.abs(cge - kf)) > 0.0)
        def _():
            t, c_ge, c_gt = _descend(low_ref, n_chunks, cgt_hi, kf,
                                     (thr_ref[1], need_ref[0], need_ref[1]),
                                     15 - LO_FIRST_BITS, 16 - LO_FIRST_BITS)
            thr_ref[1] = t
            need_ref[0] = c_ge
            need_ref[1] = c_gt

        cge, cgt = need_ref[0], need_ref[1]
        thr_ref[0] = t_hi
        thr_ref[1] = thr_ref[1] - HALF_BIAS
        need_ref[0] = kf - cgt
        need_ref[1] = cge - cgt

    t_hi16 = thr_ref[0].astype(I16)
    t_lo16 = thr_ref[1].astype(I16)

    @pl.when(jnp.max(need_ref[1] - need_ref[0]) > 0.0)
    def _():
        need = need_ref[0]
        r_i = lax.broadcasted_iota(I32, (kc, kc), 0)
        c_i = lax.broadcasted_iota(I32, (kc, kc), 1)
        tri = jnp.where(c_i <= r_i, 1.0, 0.0).astype(BF16)

        def tie_chunk(c, seen):
            eq = (hi_ref[c] == t_hi16) & (lo_ref[c] == t_lo16)
            eq_b = jnp.where(eq, jnp.bfloat16(1.0), jnp.bfloat16(0.0))
            rank = seen + _dot(tri, eq_b)
            over = jnp.where(rank > need, 1.0, 0.0).astype(BF16)
            drop = eq & (over > jnp.bfloat16(0.5))
            hi_ref[c] = jnp.where(drop, jnp.int16(NEG_INF_KEY >> 16), hi_ref[c])
            return rank[kc - 1:kc, :]

        lax.fori_loop(0, n_chunks, tie_chunk, jnp.zeros((1, Q_BLOCK), F32))

    acc_ref[...] = jnp.zeros(acc_ref.shape, F32)

    n_groups = ATT_HEADS // GROUP_HEADS
    group = lambda g: slice(g * GROUP_ROWS, (g + 1) * GROUP_ROWS)

    def keys_of(c):
        start = pl.multiple_of(c * kc, kc)
        hi, lo = hi_ref[c], lo_ref[c]
        sel = (hi > t_hi16) | ((hi == t_hi16) & (lo >= t_lo16))
        bias = jnp.where(sel, jnp.bfloat16(0.0), jnp.bfloat16(NEG_BIG))
        return jnp.concatenate([k_ref[0, pl.ds(start, kc), :], bias], axis=1)

    def softmax_step(lg_ref, g, v_c, m):
        lg = lg_ref[:, group(g)]
        m_new = jnp.maximum(m, jnp.max(lg, axis=0, keepdims=True))
        p = jnp.exp2(lg - m_new)
        acc_ref[:, group(g)] = jnp.exp2(m - m_new) * acc_ref[:, group(g)] + _dot(v_c, p.astype(BF16))
        return m_new

    last = n_chunks - 1
    lga_ref[...] = _dot_nt(keys_of(0), qs_ref[...])

    def half_step(src_ref, dst_ref, c_src, c_dst, ms):
        keys, v_c = keys_of(c_dst), vt_ref[0, c_src]
        out = []
        for g in range(n_groups):
            dst_ref[:, group(g)] = _dot_nt(keys, qs_ref[group(g), :])
            out.append(softmax_step(src_ref, g, v_c, ms[g]))
        return tuple(out)

    def attn_pair(p, ms):
        ms = half_step(lga_ref, lgb_ref, 2 * p, 2 * p + 1, ms)
        return half_step(lgb_ref, lga_ref, 2 * p + 1, jnp.minimum(2 * p + 2, last), ms)

    ms = lax.fori_loop(0, n_chunks // 2, attn_pair,
                       tuple(jnp.full((1, GROUP_ROWS), NEG_BIG, F32) for _ in range(n_groups)))

    @pl.when(n_chunks % 2 == 1)
    def _():
        v_c = vt_ref[0, last]
        for g in range(n_groups):
            softmax_step(lga_ref, g, v_c, ms[g])

    denom = acc_ref[ATT_HEAD_DIM:ATT_HEAD_DIM + 1, :]
    out = (acc_ref[0:ATT_HEAD_DIM, :] / denom).astype(BF16)
    for h in range(ATT_HEADS):
        o_ref[0, :, h * ATT_HEAD_DIM:(h + 1) * ATT_HEAD_DIM] = _dot_nt(
            eye, out[:, h * Q_BLOCK:(h + 1) * Q_BLOCK]).astype(BF16)


def _dsa(q, qi, wi_t, k, v_t, ki):
    bsz, seq, _ = q.shape
    topk = min(TOPK_MAX, seq // 4)
    assert topk % Q_BLOCK == 0 and seq % (2 * KEY_CHUNK) == 0
    assert seq // (PACK_ROWS * COUNT_ACCS) <= BF16_EXACT_INT
    n_blk = seq // Q_BLOCK
    n_ck = seq // KEY_CHUNK
    blk = lambda w: pl.BlockSpec((1, Q_BLOCK, w), lambda b, i: (b, i, 0))
    full = lambda w: pl.BlockSpec((1, seq, w), lambda b, i: (b, 0, 0))
    rows = ATT_HEADS * Q_BLOCK
    return pl.pallas_call(
        functools.partial(_dsa_kernel, topk=topk), grid=(bsz, n_blk),
        in_specs=[blk(ATT_WIDTH), blk(IDX_WIDTH),
                  pl.BlockSpec((1, IDX_HEADS, Q_BLOCK), lambda b, i: (b, 0, i)),
                  full(LANES),
                  pl.BlockSpec((1, n_ck, VT_ROWS, KEY_CHUNK), lambda b, i: (b, 0, 0, 0)),
                  full(IDX_DIM)],
        out_specs=blk(ATT_WIDTH),
        out_shape=jax.ShapeDtypeStruct((bsz, seq, ATT_WIDTH), BF16),
        scratch_shapes=[pltpu.VMEM((rows, QS_COLS), BF16),
                        pltpu.VMEM((IDX_HEADS * Q_BLOCK, IDX_DIM), BF16),
                        pltpu.VMEM((n_ck, KEY_CHUNK, Q_BLOCK), I16),
                        pltpu.VMEM((n_ck, KEY_CHUNK, Q_BLOCK), I16),
                        pltpu.VMEM((n_ck, KEY_CHUNK, Q_BLOCK), I16),
                        pltpu.VMEM((2, 1, Q_BLOCK), I32),
                        pltpu.VMEM((2, 1, Q_BLOCK), F32),
                        pltpu.VMEM((VT_ROWS, rows), F32),
                        pltpu.VMEM((KEY_CHUNK, rows), F32),
                        pltpu.VMEM((KEY_CHUNK, rows), F32)],
        compiler_params=pltpu.CompilerParams(dimension_semantics=("arbitrary", "arbitrary"),
                                             vmem_limit_bytes=VMEM_LIMIT),
        name="dsa",
    )(q, qi, wi_t, k, v_t, ki)


def _conf_in_kernel(x_ref, g_ref, w1_ref, b1_ref, cw_ref, cb_ref, lg_ref, lb_ref, o_ref, ubuf_ref, sh_ref):
    s = pl.program_id(1)
    tm = x_ref.shape[1]
    xn = _rms(x_ref[0], g_ref[...]).astype(BF16)
    z = _dot(xn, w1_ref[...]) + b1_ref[...]
    u = z[:, :CONF_WIDTH] * _sigmoid(z[:, CONF_WIDTH:])

    @pl.when(s == 0)
    def _():
        ubuf_ref[0:CONF_HALO, :] = jnp.zeros((CONF_HALO, CONF_WIDTH), F32)

    @pl.when(s > 0)
    def _():
        ubuf_ref[0:CONF_HALO, :] = ubuf_ref[tm:tm + CONF_HALO, :]

    ubuf_ref[CONF_HALO:CONF_HALO + tm, :] = u
    span = tm + CONF_HALO - SUBLANES
    for r in range(1, SUBLANES):
        sh_ref[r - 1, 0:span, :] = ubuf_ref[r:r + span, :]
    conv = u * cw_ref[CONF_KERNEL - 1:CONF_KERNEL, :] + cb_ref[...]
    for j in range(CONF_KERNEL - 1):
        base, r = divmod(CONF_HALO - (CONF_KERNEL - 1) + j, SUBLANES)
        base *= SUBLANES
        win = ubuf_ref[base:base + tm, :] if r == 0 else sh_ref[r - 1, base:base + tm, :]
        conv = conv + win * cw_ref[j:j + 1, :]
    mu = jnp.mean(conv, axis=-1, keepdims=True)
    xc = conv - mu
    var = jnp.mean(xc * xc, axis=-1, keepdims=True)
    y = xc * lax.rsqrt(var + NORM_EPS) * lg_ref[...] + lb_ref[...]
    o_ref[0] = (y * _sigmoid(y)).astype(BF16)


def _conf_in(h, g, w1, b1, conv_w, conv_b, ln_g, ln_b, tm=256):
    bsz, seq, _ = h.shape
    tm = min(tm, seq)
    tok = lambda w: pl.BlockSpec((1, tm, w), lambda b, s: (b, s, 0))
    consts = (g, w1, b1, conv_w, conv_b, ln_g, ln_b)
    return pl.pallas_call(
        _conf_in_kernel, grid=(bsz, seq // tm),
        in_specs=[tok(D_MODEL)] + [_const_spec(c.shape) for c in consts],
        out_specs=tok(CONF_WIDTH),
        out_shape=jax.ShapeDtypeStruct((bsz, seq, CONF_WIDTH), BF16),
        scratch_shapes=[pltpu.VMEM((tm + CONF_HALO, CONF_WIDTH), F32),
                        pltpu.VMEM((SUBLANES - 1, tm + CONF_HALO - SUBLANES, CONF_WIDTH), F32)],
        compiler_params=pltpu.CompilerParams(dimension_semantics=("arbitrary", "arbitrary"),
                                             vmem_limit_bytes=VMEM_LIMIT),
        name="conf_in",
    )(h, *consts)


def _rope_table(seq, head_dim):
    half = head_dim // ROT_FRACTION // 2
    inv_freq = ROPE_THETA ** (-jnp.arange(half, dtype=F32) / half)
    ang = jnp.arange(seq, dtype=F32)[:, None] * inv_freq[None, :]
    cos, sin = jnp.cos(ang), jnp.sin(ang)
    rest = head_dim - 2 * half
    cos_h = jnp.concatenate([cos, cos, jnp.ones((seq, rest), F32)], axis=1)
    sin_h = jnp.concatenate([-sin, sin, jnp.zeros((seq, rest), F32)], axis=1)
    reps = LANES // head_dim
    return jnp.tile(cos_h, (1, reps)), jnp.tile(sin_h, (1, reps))


def _pack_w_in(w_in):
    sizes = (ATT_WIDTH, ATT_HEAD_DIM, ATT_HEAD_DIM, IDX_WIDTH, IDX_DIM, IDX_HEADS)
    offs = [0]
    for sz in sizes:
        offs.append(offs[-1] + sz)
    q, k, v, qi, ki, wi = (w_in[:, offs[j]:offs[j + 1]] for j in range(len(sizes)))
    pad = lambda w: jnp.pad(w, ((0, 0), (0, LANES - w.shape[1])))
    wa = jnp.concatenate([q, qi, pad(k), pad(ki)], axis=1)
    wt = jnp.concatenate([v, wi], axis=1).T
    return wa.astype(BF16), wt.astype(BF16), w_in[:, offs[-1]:].astype(BF16)


def kernel(x, ffn_norm, ffn_w_gate, ffn_w_up, ffn_w_down, mix_norm, hyb_w_in, hyb_conv_w, hyb_w_out,
           conf_w_pw1, conf_b_pw1, conf_conv_w, conf_conv_b, conf_ln_g, conf_ln_b, conf_w_pw2,
           conf_b_pw2, final_norm):
    bsz, seq, _ = x.shape
    n = bsz * seq
    depth = ffn_norm.shape[0]
    row = lambda a: a.reshape(1, -1)
    ffn_w = lambda layer, j: (row(ffn_norm[layer, j]), ffn_w_gate[layer, j].astype(BF16),
                              ffn_w_up[layer, j].astype(BF16), ffn_w_down[layer, j].astype(BF16))
    cos_a, sin_a = _rope_table(seq, ATT_HEAD_DIM)
    cos_i, sin_i = _rope_table(seq, IDX_DIM)
    tab = jnp.concatenate([cos_a, sin_a, cos_i, sin_i], axis=1)

    h = x.reshape(n, D_MODEL)
    for layer in range(depth):
        last = layer == depth - 1
        h = _ffn(h, *ffn_w(layer, 0))
        h3 = h.reshape(bsz, seq, D_MODEL)
        if layer % 2 == 0:
            e = layer // 2
            wa, wt, wb = _pack_w_in(hyb_w_in[e])
            q, qi, k, v_t, ki, wi_t, y_conv = _mix_in(h3, row(mix_norm[layer]), wa, wt, wb,
                                                      hyb_conv_w[e], tab)
            y_attn = _dsa(q, qi, wi_t, k, v_t, ki)
            w_out = hyb_w_out[e].astype(BF16)
            pre = ((y_attn.reshape(n, ATT_WIDTH), w_out[:ATT_WIDTH]),
                   (y_conv.reshape(n, SC_WIDTH), w_out[ATT_WIDTH:]))
            bias = None
        else:
            o = layer // 2
            u = _conf_in(h3, row(mix_norm[layer]), conf_w_pw1[o].astype(BF16), row(conf_b_pw1[o]),
                         conf_conv_w[o], row(conf_conv_b[o]), row(conf_ln_g[o]), row(conf_ln_b[o]))
            pre = ((u.reshape(n, CONF_WIDTH), conf_w_pw2[o].astype(BF16)),)
            bias = row(conf_b_pw2[o])
        h = _ffn(h, *ffn_w(layer, 1), pre=pre, bias=bias,
                 final_g=row(final_norm) if last else None)
    return h.reshape(bsz, seq, D_MODEL)
```
